```python
import math, functools
import jax, jax.numpy as jnp
from jax import lax
import numpy as np

D_MODEL = 2048
BATCH = 8
SEQ = 2048
DEPTH = 1
DEC_BATCH = 128
DEC_SEQ = 1
PAST_LEN = 2048
PAGE_SIZE = 128

N_HEADS = 8
HEAD_DIM = 128
V_DIM = 2 * HEAD_DIM
QK_WIDTH = N_HEADS * 2 * HEAD_DIM
ATTN_WIDTH = N_HEADS * V_DIM
LRU_WIDTH = D_MODEL
LRU_BLOCKS = 8
LRU_BLOCK = LRU_WIDTH // LRU_BLOCKS
CONV_WIDTH = 4
LRU_C = 8.0
D_FF = ((8 * D_MODEL // 3 + 255) // 256) * 256
ROPE_THETA = 10000.0
EPS = 1e-6
Q_BLOCK = 128
IN_SPLITS = [QK_WIDTH, 2 * QK_WIDTH, 2 * QK_WIDTH + ATTN_WIDTH,
             2 * QK_WIDTH + ATTN_WIDTH + LRU_WIDTH, 2 * QK_WIDTH + ATTN_WIDTH + 2 * LRU_WIDTH]
IN_WIDTH = 2 * QK_WIDTH + ATTN_WIDTH + 2 * LRU_WIDTH + 2 * D_MODEL

kernel_name = 'diff_attn_rglru_macaron_decode_step'


def _rmsnorm(x, g):
    xf = x.astype(jnp.float32)
    y = xf * lax.rsqrt(jnp.mean(xf * xf, axis=-1, keepdims=True) + EPS)
    return (y * g.astype(jnp.float32)).astype(x.dtype)


def _lambda_init(layer):
    return 0.8 - 0.6 * math.exp(-0.3 * layer)


def _rope(x, positions):
    half = HEAD_DIM // 2
    inv = 1.0 / (ROPE_THETA ** (jnp.arange(half, dtype=jnp.float32) / half))
    ang = positions.astype(jnp.float32)[:, None] * inv[None, :]
    cos = jnp.cos(ang)[:, None, None, :]
    sin = jnp.sin(ang)[:, None, None, :]
    xf = x.astype(jnp.float32)
    x1, x2 = xf[..., :half], xf[..., half:]
    return jnp.concatenate([x1 * cos - x2 * sin, x2 * cos + x1 * sin], axis=-1).astype(x.dtype)


def _swiglu(x, w_up, w_down):
    g, u = jnp.split(x @ w_up, 2, axis=-1)
    return (jax.nn.silu(g) * u) @ w_down


def _attn_prompt(q, k, v):
    B, S = q.shape[0], q.shape[1]
    scale = HEAD_DIM ** -0.5
    kf = k.astype(jnp.float32)
    vf = v.astype(jnp.float32)
    kpos = jnp.arange(S)

    def block(i):
        qb = lax.dynamic_slice_in_dim(q, i * Q_BLOCK, Q_BLOCK, axis=1).astype(jnp.float32)
        s = jnp.einsum('bqhcd,bkhcd->bhcqk', qb, kf) * scale
        qpos = i * Q_BLOCK + jnp.arange(Q_BLOCK)
        s = jnp.where(qpos[:, None] >= kpos[None, :], s, -jnp.inf)
        p = jax.nn.softmax(s, axis=-1)
        return jnp.einsum('bhcqk,bkhe->bqhce', p, vf)

    o = lax.map(block, jnp.arange(S // Q_BLOCK))
    return jnp.moveaxis(o, 0, 1).reshape(B, S, N_HEADS, 2, V_DIM)


def _online(carry, s, v):
    m, l, acc = carry
    m_new = jnp.maximum(m, s.max(axis=-1))
    alpha = jnp.exp(m - m_new)
    p = jnp.exp(s - m_new[..., None])
    l = alpha * l + p.sum(axis=-1)
    acc = alpha[..., None] * acc + jnp.einsum('bhctk,bkhe->bhcte', p, v)
    return (m_new, l, acc)


def _attn_sample(q, k, v, cache_k, cache_v, page_table, layer):
    B, T = q.shape[0], q.shape[1]
    scale = HEAD_DIM ** -0.5
    qf = q.astype(jnp.float32)
    init = (jnp.full((B, N_HEADS, 2, T), -jnp.inf, jnp.float32),
            jnp.zeros((B, N_HEADS, 2, T), jnp.float32),
            jnp.zeros((B, N_HEADS, 2, T, V_DIM), jnp.float32))

    def page_step(carry, pages):
        kp = cache_k[layer, pages].astype(jnp.float32).reshape(B, PAGE_SIZE, N_HEADS, 2, HEAD_DIM)
        vp = cache_v[layer, pages].astype(jnp.float32)
        s = jnp.einsum('bthcd,bkhcd->bhctk', qf, kp) * scale
        return _online(carry, s, vp), None

    carry, _ = lax.scan(page_step, init, page_table.T)
    s = jnp.einsum('bthcd,bkhcd->bhctk', qf, k.astype(jnp.float32)) * scale
    causal = jnp.arange(T)[:, None] >= jnp.arange(T)[None, :]
    s = jnp.where(causal, s, -jnp.inf)
    m, l, acc = _online(carry, s, v.astype(jnp.float32))
    o = acc / l[..., None]
    return jnp.transpose(o, (0, 3, 1, 2, 4))


def _causal_conv(x, buf, w, b):
    T = x.shape[1]
    xx = jnp.concatenate([buf.astype(x.dtype), x], axis=1)
    y = b + sum(xx[:, i:i + T] * w[i] for i in range(CONV_WIDTH))
    return y, xx[:, -(CONV_WIDTH - 1):]


def _rglru(x, h0, w_a, b_a, w_x, b_x, a_param, positions):
    B, T, W = x.shape
    xb = x.reshape(B, T, LRU_BLOCKS, LRU_BLOCK)
    gate_a = jax.nn.sigmoid(jnp.einsum('btnd,nde->btne', xb, w_a).reshape(B, T, W) + b_a)
    gate_x = jax.nn.sigmoid(jnp.einsum('btnd,nde->btne', xb, w_x).reshape(B, T, W) + b_x)
    log_a = -LRU_C * gate_a.astype(jnp.float32) * jax.nn.softplus(-a_param.astype(jnp.float32))
    a = jnp.exp(log_a)
    mult = jnp.sqrt(-jnp.expm1(2.0 * log_a))
    mult = jnp.where((positions == 0)[None, :, None], 1.0, mult)
    bx = x.astype(jnp.float32) * gate_x.astype(jnp.float32) * mult

    def step(h, ab):
        a_t, b_t = ab
        h = a_t * h + b_t
        return h, h

    h_last, hs = lax.scan(step, h0.astype(jnp.float32), (jnp.moveaxis(a, 1, 0), jnp.moveaxis(bx, 1, 0)))
    return jnp.moveaxis(hs, 0, 1).astype(x.dtype), h_last


def _layer(x, positions, attend, conv_buf, h0, lam_init, lw):
    (g_ffn1, w_up1, w_down1, g_mix, w_in, g_q, g_k, lam_q1, lam_k1, lam_q2, lam_k2, g_sub,
     conv_w, conv_b, w_a, b_a, w_x, b_x, a_param, w_pa, w_pb, w_o, g_ffn2, w_up2, w_down2) = lw
    B, T, _ = x.shape
    f32 = jnp.float32
    x = x + 0.5 * _swiglu(_rmsnorm(x, g_ffn1), w_up1, w_down1)
    h = _rmsnorm(x, g_mix)
    q, k, v, xr, xg, gates = jnp.split(h @ w_in, IN_SPLITS, axis=-1)
    q = _rope(_rmsnorm(q.reshape(B, T, N_HEADS, 2, HEAD_DIM), g_q), positions)
    k = _rope(_rmsnorm(k.reshape(B, T, N_HEADS, 2, HEAD_DIM), g_k), positions)
    v = v.reshape(B, T, N_HEADS, V_DIM)
    lam = (jnp.exp(jnp.sum(lam_q1.astype(f32) * lam_k1.astype(f32)))
           - jnp.exp(jnp.sum(lam_q2.astype(f32) * lam_k2.astype(f32))) + lam_init)
    o = attend(q, k, v)
    d = o[..., 0, :] - lam * o[..., 1, :]
    y_a = (_rmsnorm(d, g_sub) * (1.0 - lam_init)).reshape(B, T, ATTN_WIDTH).astype(x.dtype) @ w_pa
    xc, conv_new = _causal_conv(xr, conv_buf, conv_w, conv_b)
    lru, h_new = _rglru(xc, h0, w_a, b_a, w_x, b_x, a_param, positions)
    y_b = (lru * jax.nn.gelu(xg)) @ w_pb
    g_a, g_b = jnp.split(jax.nn.sigmoid(gates), 2, axis=-1)
    x = x + (g_a * y_a + g_b * y_b) @ w_o
    x = x + 0.5 * _swiglu(_rmsnorm(x, g_ffn2), w_up2, w_down2)
    return x, k.reshape(B, T, N_HEADS, 2 * HEAD_DIM), v, conv_new, h_new.astype(h0.dtype)


def setup_inputs(seed: int = 0) -> dict:
    key = jax.random.key(seed)
    ks = iter(jax.random.split(key, 48))
    f32 = jnp.float32

    def nrm(shape, scale=1.0):
        return jax.random.normal(next(ks), shape, f32) * scale

    def gain(shape):
        return 1.0 + nrm(shape, 0.02)

    n_pages = PAST_LEN // PAGE_SIZE
    n_used = DEC_BATCH * n_pages
    n_pool = n_used + n_used // 4
    L = DEPTH
    x_prompt = nrm((BATCH, SEQ, D_MODEL))
    x_sample = nrm((DEC_BATCH, DEC_SEQ, D_MODEL))
    cache_k = nrm((L, n_pool, PAGE_SIZE, N_HEADS, 2 * HEAD_DIM))
    cache_v = nrm((L, n_pool, PAGE_SIZE, N_HEADS, V_DIM))
    state_conv = nrm((L, DEC_BATCH, CONV_WIDTH - 1, LRU_WIDTH))
    state_h = nrm((L, DEC_BATCH, LRU_WIDTH), 0.5)
    page_table = jax.random.permutation(next(ks), n_pool)[:n_used].reshape(DEC_BATCH, n_pages).astype(jnp.int32)
    u = jax.random.uniform(next(ks), (L, LRU_WIDTH), f32, 0.9, 0.999)
    s = u ** (1.0 / LRU_C)
    a_param = jnp.log(s) - jnp.log1p(-s)
    return {
        'x_prompt': x_prompt, 'x_sample': x_sample,
        'cache_k': cache_k, 'cache_v': cache_v,
        'state_conv': state_conv, 'state_h': state_h, 'page_table': page_table,
        'g_ffn1': gain((L, D_MODEL)),
        'w_up1': nrm((L, D_MODEL, 2 * D_FF), D_MODEL ** -0.5),
        'w_down1': nrm((L, D_FF, D_MODEL), D_FF ** -0.5),
        'g_mix': gain((L, D_MODEL)),
        'w_in': nrm((L, D_MODEL, IN_WIDTH), D_MODEL ** -0.5),
        'g_q': gain((L, HEAD_DIM)), 'g_k': gain((L, HEAD_DIM)),
        'lam_q1': nrm((L, HEAD_DIM), 0.1), 'lam_k1': nrm((L, HEAD_DIM), 0.1),
        'lam_q2': nrm((L, HEAD_DIM), 0.1), 'lam_k2': nrm((L, HEAD_DIM), 0.1),
        'g_sub': gain((L, V_DIM)),
        'conv_w': nrm((L, CONV_WIDTH, LRU_WIDTH), CONV_WIDTH ** -0.5),
        'conv_b': nrm((L, LRU_WIDTH), 0.02),
        'w_a': nrm((L, LRU_BLOCKS, LRU_BLOCK, LRU_BLOCK), LRU_BLOCK ** -0.5),
        'b_a': nrm((L, LRU_WIDTH), 0.02),
        'w_x': nrm((L, LRU_BLOCKS, LRU_BLOCK, LRU_BLOCK), LRU_BLOCK ** -0.5),
        'b_x': nrm((L, LRU_WIDTH), 0.02),
        'a_param': a_param,
        'w_pa': nrm((L, ATTN_WIDTH, D_MODEL), ATTN_WIDTH ** -0.5),
        'w_pb': nrm((L, LRU_WIDTH, D_MODEL), LRU_WIDTH ** -0.5),
        'w_o': nrm((L, D_MODEL, D_MODEL), D_MODEL ** -0.5),
        'g_ffn2': gain((L, D_MODEL)),
        'w_up2': nrm((L, D_MODEL, 2 * D_FF), D_MODEL ** -0.5),
        'w_down2': nrm((L, D_FF, D_MODEL), D_FF ** -0.5),
    }


def reference(x_prompt, x_sample, cache_k, cache_v, state_conv, state_h, page_table,
              g_ffn1, w_up1, w_down1, g_mix, w_in, g_q, g_k, lam_q1, lam_k1, lam_q2, lam_k2, g_sub,
              conv_w, conv_b, w_a, b_a, w_x, b_x, a_param, w_pa, w_pb, w_o, g_ffn2, w_up2, w_down2):
    weights = (g_ffn1, w_up1, w_down1, g_mix, w_in, g_q, g_k, lam_q1, lam_k1, lam_q2, lam_k2, g_sub,
               conv_w, conv_b, w_a, b_a, w_x, b_x, a_param, w_pa, w_pb, w_o, g_ffn2, w_up2, w_down2)
    B, S = x_prompt.shape[0], x_prompt.shape[1]
    T = x_sample.shape[1]
    pos_p = jnp.arange(S)
    pos_s = PAST_LEN + jnp.arange(T)
    yp, ys = x_prompt, x_sample
    kp_l, vp_l, cp_l, hp_l, ks_l, vs_l, cs_l, hs_l = [], [], [], [], [], [], [], []
    for l in range(DEPTH):
        lw = tuple(w[l] for w in weights)
        lam_init = _lambda_init(l)
        yp, kp, vp, cp, hp = _layer(
            yp, pos_p, _attn_prompt,
            jnp.zeros((B, CONV_WIDTH - 1, LRU_WIDTH), x_prompt.dtype),
            jnp.zeros((B, LRU_WIDTH), x_prompt.dtype), lam_init, lw)
        attend_s = functools.partial(_attn_sample, cache_k=cache_k, cache_v=cache_v,
                                     page_table=page_table, layer=l)
        ys, k_s, v_s, c_s, h_s = _layer(ys, pos_s, attend_s, state_conv[l], state_h[l], lam_init, lw)
        kp_l.append(kp); vp_l.append(vp); cp_l.append(cp); hp_l.append(hp)
        ks_l.append(k_s); vs_l.append(v_s); cs_l.append(c_s); hs_l.append(h_s)
    return (yp, ys,
            jnp.stack(kp_l), jnp.stack(vp_l), jnp.stack(cp_l), jnp.stack(hp_l),
            jnp.stack(ks_l), jnp.stack(vs_l), jnp.stack(cs_l), jnp.stack(hs_l))
```

```python
import functools
import math

import jax
import jax.numpy as jnp
from jax import lax
from jax.experimental import pallas as pl
from jax.experimental.pallas import tpu as pltpu

EPS = 1e-6
LRU_C = 8.0
ROPE_THETA = 10000.0
LANES = 128
SUBLANES = 8
VMEM_LIMIT_BYTES = 56 * 1024 * 1024

f32 = jnp.float32
bf16 = jnp.bfloat16


def _cparams(*sem):
    return pltpu.CompilerParams(dimension_semantics=sem, vmem_limit_bytes=VMEM_LIMIT_BYTES)


def _rms(x, g):
    return x * lax.rsqrt(jnp.mean(x * x, axis=-1, keepdims=True) + EPS) * g


def _gelu_tanh(x):
    c = math.sqrt(2.0 / math.pi)
    return 0.5 * x * (1.0 + jnp.tanh(c * (x + 0.044715 * (x * x * x))))


def _pick(n, pref):
    if n <= pref:
        return n
    b = pref
    while n % b:
        b //= 2
    return b


def _ffn_kernel(x_ref, g_ref, wg_ref, wu_ref, wd_ref, gn_ref, *rest, nj, emit_norm):
    if emit_norm:
        o_ref, hn_ref, h_ref = rest
    else:
        o_ref, h_ref = rest
    j = pl.program_id(1)

    @pl.when(j == 0)
    def _():
        x = x_ref[...]
        h_ref[...] = _rms(x, g_ref[...]).astype(bf16)
        o_ref[...] = x

    h = h_ref[...]
    gg = jnp.dot(h, wg_ref[...], preferred_element_type=f32)
    uu = jnp.dot(h, wu_ref[...], preferred_element_type=f32)
    act = (gg * jax.nn.sigmoid(gg)) * (uu * 0.5)
    o_ref[...] += jnp.dot(act.astype(bf16), wd_ref[...], preferred_element_type=f32)

    if emit_norm:
        @pl.when(j == nj - 1)
        def _():
            hn_ref[...] = _rms(o_ref[...], gn_ref[...]).astype(bf16)


def _ffn(x, g, w_up, w_down, g_next=None, *, bm=512, bf=512):
    m, d = x.shape
    ff = w_down.shape[0]
    bm = _pick(m, bm)
    bf = _pick(ff, bf)
    nj = ff // bf
    emit_norm = g_next is not None
    gn = g_next if emit_norm else g
    out_shape = [jax.ShapeDtypeStruct((m, d), f32)]
    out_specs = [pl.BlockSpec((bm, d), lambda i, j: (i, 0))]
    if emit_norm:
        out_shape.append(jax.ShapeDtypeStruct((m, d), bf16))
        out_specs.append(pl.BlockSpec((bm, d), lambda i, j: (i, 0)))
    res = pl.pallas_call(
        functools.partial(_ffn_kernel, nj=nj, emit_norm=emit_norm),
        grid=(m // bm, nj),
        in_specs=[
            pl.BlockSpec((bm, d), lambda i, j: (i, 0)),
            pl.BlockSpec((1, d), lambda i, j: (0, 0)),
            pl.BlockSpec((d, bf), lambda i, j: (0, j)),
            pl.BlockSpec((d, bf), lambda i, j: (0, j + nj)),
            pl.BlockSpec((bf, d), lambda i, j: (j, 0)),
            pl.BlockSpec((1, d), lambda i, j: (0, 0)),
        ],
        out_specs=out_specs,
        out_shape=out_shape,
        scratch_shapes=[pltpu.VMEM((bm, d), bf16)],
        compiler_params=_cparams("parallel", "arbitrary"),
        name="ffn",
    )(x, g, w_up, w_up, w_down, gn)
    return res if emit_norm else res[0]


def _qk_norm_rope(z, g, cos, sin, scale):
    outs = []
    for n in range(z.shape[1] // LANES):
        x = z[:, n * LANES:(n + 1) * LANES]
        y = _rms(x, g)
        r = y * cos + pltpu.roll(y, LANES // 2, axis=1) * sin
        if scale != 1.0:
            r = r * scale
        outs.append(r)
    return outs


def _proj_qk_kernel(a_ref, w_ref, g_ref, cos_ref, sin_ref, *o_refs, scale, out_dtypes):
    z = jnp.dot(a_ref[...], w_ref[...], preferred_element_type=f32)
    chunks = _qk_norm_rope(z, g_ref[...], cos_ref[...], sin_ref[...], scale)
    for n, r in enumerate(chunks):
        for o_ref, dt in zip(o_refs, out_dtypes):
            o_ref[:, n * LANES:(n + 1) * LANES] = r.astype(dt)


def _proj_plain_kernel(a_ref, w_ref, *o_refs, act, out_dtypes):
    z = jnp.dot(a_ref[...], w_ref[...], preferred_element_type=f32)
    if act == "sigmoid":
        z = jax.nn.sigmoid(z)
    for o_ref, dt in zip(o_refs, out_dtypes):
        o_ref[...] = z.astype(dt)


def _proj(hn, w_in, col0, ncols, out_dtypes, *, qk=None, act=None, bm=1024, bn=512):
    m, d = hn.shape
    bm = _pick(m, bm)
    bn = _pick(ncols, bn)
    joff = col0 // bn
    assert col0 % bn == 0
    in_specs = [
        pl.BlockSpec((bm, d), lambda i, j: (i, 0)),
        pl.BlockSpec((d, bn), lambda i, j: (0, j + joff)),
    ]
    args = [hn, w_in]
    if qk is not None:
        g, cos, sin, scale = qk
        p = cos.shape[0]
        bm = _pick(p, bm)
        in_specs[0] = pl.BlockSpec((bm, d), lambda i, j: (i, 0))
        npos = p // bm
        in_specs += [
            pl.BlockSpec((1, LANES), lambda i, j: (0, 0)),
            pl.BlockSpec((bm, LANES), lambda i, j: (i % npos, 0)),
            pl.BlockSpec((bm, LANES), lambda i, j: (i % npos, 0)),
        ]
        args += [g, cos, sin]
        body = functools.partial(_proj_qk_kernel, scale=scale, out_dtypes=out_dtypes)
    else:
        body = functools.partial(_proj_plain_kernel, act=act, out_dtypes=out_dtypes)
    return pl.pallas_call(
        body,
        grid=(m // bm, ncols // bn),
        in_specs=in_specs,
        out_specs=[pl.BlockSpec((bm, bn), lambda i, j: (i, j)) for _ in out_dtypes],
        out_shape=[jax.ShapeDtypeStruct((m, ncols), dt) for dt in out_dtypes],
        compiler_params=_cparams("parallel", "arbitrary"),
        name="proj",
    )(*args)


def _lambda(lam_ref, lam_init):
    r = lam_ref[...]
    s1 = jnp.sum(r[0:1] * r[1:2], axis=-1, keepdims=True)
    s2 = jnp.sum(r[2:3] * r[3:4], axis=-1, keepdims=True)
    return jnp.exp(s1) - jnp.exp(s2) + lam_init


def _attn_prompt_kernel(lam_ref, gsub_ref, q_ref, k_ref, v_ref, o_ref,
                        m_ref, l_ref, acc_ref, o1_ref, *, tq, hd, lam_init):
    qi = pl.program_id(2)
    vd = v_ref.shape[1]
    nrep_s = tq // LANES
    nrep_v = vd // LANES

    def kv_block(j, c, masked):
        r0 = pl.multiple_of(j * tq, tq)
        q = q_ref[:, c * hd:(c + 1) * hd]
        k = k_ref[pl.ds(r0, tq), c * hd:(c + 1) * hd]
        v = v_ref[pl.ds(r0, tq), :]
        s = lax.dot_general(q, k, (((1,), (1,)), ((), ())), preferred_element_type=f32)
        if masked:
            row = lax.broadcasted_iota(jnp.int32, s.shape, 0)
            col = lax.broadcasted_iota(jnp.int32, s.shape, 1)
            s = jnp.where(row >= col, s, -jnp.inf)
        m_prev = m_ref[...]
        m_new = jnp.maximum(m_prev, jnp.max(s, axis=1, keepdims=True))
        alpha = jnp.exp(m_prev - m_new)
        p = jnp.exp(s - pltpu.repeat(m_new, nrep_s, axis=1))
        l_ref[...] = alpha * l_ref[...] + jnp.sum(p, axis=1, keepdims=True)
        acc_ref[...] = (acc_ref[...] * pltpu.repeat(alpha, nrep_v, axis=1)
                        + jnp.dot(p.astype(bf16), v, preferred_element_type=f32))
        m_ref[...] = m_new

    for c in range(2):
        m_ref[...] = jnp.full(m_ref.shape, -jnp.inf, f32)
        l_ref[...] = jnp.zeros(l_ref.shape, f32)
        acc_ref[...] = jnp.zeros(acc_ref.shape, f32)

        def body(j, carry, c=c):
            kv_block(j, c, False)
            return carry

        lax.fori_loop(0, qi, body, 0)
        kv_block(qi, c, True)
        o = acc_ref[...] / pltpu.repeat(l_ref[...], nrep_v, axis=1)
        if c == 0:
            o1_ref[...] = o
        else:
            lam = _lambda(lam_ref, lam_init)
            d = o1_ref[...] - lam * o
            o_ref[...] = (_rms(d, gsub_ref[...]) * (1.0 - lam_init)).astype(o_ref.dtype)


def _attn_prompt(q, k, v, lam4, g_sub, *, batch, seq, n_heads, hd, vd, lam_init, tq=512):
    tq = _pick(seq, tq)
    nq = seq // tq
    return pl.pallas_call(
        functools.partial(_attn_prompt_kernel, tq=tq, hd=hd, lam_init=lam_init),
        grid=(batch, n_heads, nq),
        in_specs=[
            pl.BlockSpec((4, hd), lambda b, h, i: (0, 0)),
            pl.BlockSpec((1, vd), lambda b, h, i: (0, 0)),
            pl.BlockSpec((tq, 2 * hd), lambda b, h, i: (b * nq + i, h)),
            pl.BlockSpec((seq, 2 * hd), lambda b, h, i: (b, h)),
            pl.BlockSpec((seq, vd), lambda b, h, i: (b, h)),
        ],
        out_specs=pl.BlockSpec((tq, vd), lambda b, h, i: (b * nq + i, h)),
        out_shape=jax.ShapeDtypeStruct((batch * seq, n_heads * vd), bf16),
        scratch_shapes=[
            pltpu.VMEM((tq, LANES), f32),
            pltpu.VMEM((tq, LANES), f32),
            pltpu.VMEM((tq, vd), f32),
            pltpu.VMEM((tq, vd), f32),
        ],
        compiler_params=_cparams("parallel", "parallel", "arbitrary"),
        name="attn_prompt",
    )(lam4, g_sub, q, k, v)


def _attn_decode_kernel(pt_ref, lam_ref, gsub_ref, ones_ref, q_ref, kn_ref, vn_ref, *rest,
                        pp, ck, hd, lam_init):
    k_refs = rest[:pp]
    v_refs = rest[pp:2 * pp]
    o_ref, m_ref, l_ref, acc1_ref, acc2_ref = rest[2 * pp:]
    g = pl.program_id(1)
    ng = pl.num_programs(1)
    nh = q_ref.shape[1]

    @pl.when(g == 0)
    def _():
        m_ref[...] = jnp.full(m_ref.shape, -jnp.inf, f32)
        l_ref[...] = jnp.zeros(l_ref.shape, f32)
        acc1_ref[...] = jnp.zeros(acc1_ref.shape, f32)
        acc2_ref[...] = jnp.zeros(acc2_ref.shape, f32)

    q = q_ref[0]
    ones = ones_ref[...]

    def both(x):
        a, b = x[:, :hd], x[:, hd:]
        return jnp.concatenate([a, a], axis=1), jnp.concatenate([b, b], axis=1)

    def update(k, v, nk, carry):
        m, l, a1, a2 = carry
        qt = jnp.concatenate([q] * nk, axis=0) if nk > 1 else q
        s = jnp.dot((k * qt).astype(bf16), ones, preferred_element_type=f32)
        mc = s[0:nh]
        for i in range(1, nk):
            mc = jnp.maximum(mc, s[i * nh:(i + 1) * nh])
        m_new = jnp.maximum(m, mc)
        alpha = jnp.exp(m - m_new)
        al1, al2 = both(alpha)
        l = alpha * l
        a1 = al1 * a1
        a2 = al2 * a2
        for i in range(nk):
            p = jnp.exp(s[i * nh:(i + 1) * nh] - m_new)
            p1, p2 = both(p)
            vi = v[i * nh:(i + 1) * nh]
            l = l + p
            a1 = a1 + p1 * vi
            a2 = a2 + p2 * vi
        return m_new, l, a1, a2

    carry = (m_ref[...], l_ref[...], acc1_ref[...], acc2_ref[...])
    rows = ck * nh
    for pi in range(pp):
        k_ref, v_ref = k_refs[pi], v_refs[pi]
        nchunks = k_ref.shape[2] // rows

        def body(c, carry, k_ref=k_ref, v_ref=v_ref):
            r0 = pl.multiple_of(c * rows, rows)
            return update(k_ref[0, 0, pl.ds(r0, rows), :], v_ref[0, 0, pl.ds(r0, rows), :], ck, carry)

        carry = lax.fori_loop(0, nchunks, body, carry)

    m_ref[...], l_ref[...], acc1_ref[...], acc2_ref[...] = carry

    @pl.when(g == ng - 1)
    def _():
        m, l, a1, a2 = update(kn_ref[0], vn_ref[0], 1, carry)
        l1, l2 = both(l)
        lam = _lambda(lam_ref, lam_init)
        d = a1 / l1 - lam * (a2 / l2)
        o_ref[0] = (_rms(d, gsub_ref[...]) * (1.0 - lam_init)).astype(o_ref.dtype)


def _attn_decode(q, k_new, v_new, cache_k, cache_v, page_table, lam4, g_sub, *, layer, hd,
                 lam_init, pp=4, ck=16):
    b, nh, w = q.shape
    n_pages = page_table.shape[1]
    pp = _pick(n_pages, pp)
    page_rows = cache_k.shape[2]
    ck = _pick(page_rows // nh, ck)
    ones = jnp.kron(jnp.eye(w // hd, dtype=f32), jnp.ones((hd, hd), f32)).astype(bf16)

    def page_spec(pi):
        return pl.BlockSpec((1, 1, page_rows, w),
                            lambda i, g, pt, pi=pi: (layer, pt[i, g * pp + pi], 0, 0))

    row_spec = pl.BlockSpec((1, nh, w), lambda i, g, pt: (i, 0, 0))
    grid_spec = pltpu.PrefetchScalarGridSpec(
        num_scalar_prefetch=1,
        grid=(b, n_pages // pp),
        in_specs=[
            pl.BlockSpec((4, hd), lambda i, g, pt: (0, 0)),
            pl.BlockSpec((1, w), lambda i, g, pt: (0, 0)),
            pl.BlockSpec((w, w), lambda i, g, pt: (0, 0)),
            row_spec, row_spec, row_spec,
        ] + [page_spec(pi) for pi in range(pp)] * 2,
        out_specs=row_spec,
        scratch_shapes=[pltpu.VMEM((nh, w), f32)] * 4,
    )
    return pl.pallas_call(
        functools.partial(_attn_decode_kernel, pp=pp, ck=ck, hd=hd, lam_init=lam_init),
        grid_spec=grid_spec,
        out_shape=jax.ShapeDtypeStruct((b, nh, w), bf16),
        compiler_params=_cparams("parallel", "arbitrary"),
        name="attn_decode",
    )(page_table, lam4, g_sub, ones, q, k_new, v_new,
      *([cache_k] * pp), *([cache_v] * pp))


def _lru_gates(xc, wa_ref, ba_ref, wx_ref, bx_ref, ap_ref):
    nb, blk = wa_ref.shape[0], wa_ref.shape[1]
    xcb = xc.astype(bf16)
    za = jnp.concatenate(
        [jnp.dot(xcb[:, n * blk:(n + 1) * blk], wa_ref[n], preferred_element_type=f32)
         for n in range(nb)], axis=1)
    zx = jnp.concatenate(
        [jnp.dot(xcb[:, n * blk:(n + 1) * blk], wx_ref[n], preferred_element_type=f32)
         for n in range(nb)], axis=1)
    gate_a = jax.nn.sigmoid(za + ba_ref[...])
    gate_x = jax.nn.sigmoid(zx + bx_ref[...])
    ap = -ap_ref[...]
    softplus = jnp.maximum(ap, 0.0) + jnp.log1p(jnp.exp(-jnp.abs(ap)))
    log_a = (-LRU_C) * gate_a * softplus
    a = jnp.exp(log_a)
    mult = jnp.sqrt(-jnp.tanh(log_a) * (a * a + 1.0))
    return a, gate_x, mult


def _lru_prompt_kernel(xr_ref, xg_ref, cw_ref, cb_ref, wa_ref, ba_ref, wx_ref, bx_ref, ap_ref,
                       o_ref, conv_ref, h_ref, xpad_ref, hc_ref, a_s, b_s, *, tt, pos0):
    t = pl.program_id(1)
    nt = pl.num_programs(1)
    w = xr_ref.shape[1]
    cwid = cw_ref.shape[0]

    @pl.when(t == 0)
    def _():
        xpad_ref[0:SUBLANES, :] = jnp.zeros((SUBLANES, w), f32)
        hc_ref[...] = jnp.zeros(hc_ref.shape, f32)

    x = xr_ref[...]
    xpad_ref[SUBLANES:SUBLANES + tt, :] = x
    xc = x * cw_ref[cwid - 1:cwid, :]
    for i in range(1, cwid):
        xc = xc + xpad_ref[SUBLANES - i:SUBLANES - i + tt, :] * cw_ref[cwid - 1 - i:cwid - i, :]
    xc = xc + cb_ref[...]
    xpad_ref[0:SUBLANES, :] = xpad_ref[tt:tt + SUBLANES, :]

    a, gate_x, mult = _lru_gates(xc, wa_ref, ba_ref, wx_ref, bx_ref, ap_ref)
    row = lax.broadcasted_iota(jnp.int32, (tt, w), 0)
    if pos0 == 0:
        mult = jnp.where((row + t * tt) == 0, 1.0, mult)
    b = xc * gate_x * mult

    sub = row % SUBLANES
    for s in (1, 2, 4):
        keep = sub >= s
        b = jnp.where(keep, a * pltpu.roll(b, s, axis=0) + b, b)
        a = jnp.where(keep, a * pltpu.roll(a, s, axis=0), a)
    a_s[...] = a
    b_s[...] = b

    def grp(gi, h):
        r0 = pl.multiple_of(gi * SUBLANES, SUBLANES)
        hg = a_s[pl.ds(r0, SUBLANES), :] * h + b_s[pl.ds(r0, SUBLANES), :]
        b_s[pl.ds(r0, SUBLANES), :] = hg
        return jnp.broadcast_to(hg[SUBLANES - 1:SUBLANES, :], (SUBLANES, w))

    h = lax.fori_loop(0, tt // SUBLANES, grp, hc_ref[...])
    hc_ref[...] = h
    o_ref[...] = (b_s[...] * _gelu_tanh(xg_ref[...])).astype(o_ref.dtype)

    @pl.when(t == nt - 1)
    def _():
        conv_ref[0] = xpad_ref[SUBLANES - (cwid - 1):SUBLANES, :]
        h_ref[0] = h[0:1, :]


def _lru_prompt(xrg, conv_w, conv_b, w_a, b_a, w_x, b_x, a_param, *, batch, seq, pos0, tt=256):
    w = xrg.shape[1] // 2
    tt = _pick(seq, tt)
    nt = seq // tt
    cwid = conv_w.shape[0]
    vec = pl.BlockSpec((1, w), lambda b, t: (0, 0))
    wspec = pl.BlockSpec(w_a.shape, lambda b, t: (0, 0, 0))
    return pl.pallas_call(
        functools.partial(_lru_prompt_kernel, tt=tt, pos0=pos0),
        grid=(batch, nt),
        in_specs=[
            pl.BlockSpec((tt, w), lambda b, t: (b * nt + t, 0)),
            pl.BlockSpec((tt, w), lambda b, t: (b * nt + t, 1)),
            pl.BlockSpec((cwid, w), lambda b, t: (0, 0)),
            vec, wspec, vec, wspec, vec, vec,
        ],
        out_specs=[
            pl.BlockSpec((tt, w), lambda b, t: (b * nt + t, 0)),
            pl.BlockSpec((1, cwid - 1, w), lambda b, t: (b, 0, 0)),
            pl.BlockSpec((1, 1, w), lambda b, t: (b, 0, 0)),
        ],
        out_shape=[
            jax.ShapeDtypeStruct((batch * seq, w), bf16),
            jax.ShapeDtypeStruct((batch, cwid - 1, w), f32),
            jax.ShapeDtypeStruct((batch, 1, w), f32),
        ],
        scratch_shapes=[
            pltpu.VMEM((tt + SUBLANES, w), f32),
            pltpu.VMEM((SUBLANES, w), f32),
            pltpu.VMEM((tt, w), f32),
            pltpu.VMEM((tt, w), f32),
        ],
        compiler_params=_cparams("parallel", "arbitrary"),
        name="lru_prompt",
    )(xrg, xrg, conv_w, conv_b, w_a, b_a, w_x, b_x, a_param)


def _lru_step_kernel(xr_ref, xg_ref, buf_ref, h0_ref, cw_ref, cb_ref, wa_ref, ba_ref, wx_ref,
                     bx_ref, ap_ref, o_ref, conv_ref, h_ref, *, pos0):
    cwid = cw_ref.shape[0]
    x = xr_ref[...]
    xc = cb_ref[...] + x * cw_ref[cwid - 1:cwid, :]
    for i in range(cwid - 1):
        xc = xc + buf_ref[i] * cw_ref[i:i + 1, :]
        if i > 0:
            conv_ref[i - 1] = buf_ref[i]
    conv_ref[cwid - 2] = x
    a, gate_x, mult = _lru_gates(xc, wa_ref, ba_ref, wx_ref, bx_ref, ap_ref)
    if pos0 == 0:
        mult = jnp.ones_like(mult)
    h = a * h0_ref[...] + xc * gate_x * mult
    h_ref[...] = h
    o_ref[...] = (h * _gelu_tanh(xg_ref[...])).astype(o_ref.dtype)


def _lru_step(xrg, buf, h0, conv_w, conv_b, w_a, b_a, w_x, b_x, a_param, *, pos0):
    b = xrg.shape[0]
    w = xrg.shape[1] // 2
    cwid = conv_w.shape[0]
    vec = pl.BlockSpec((1, w), lambda i: (0, 0))
    wspec = pl.BlockSpec(w_a.shape, lambda i: (0, 0, 0))
    return pl.pallas_call(
        functools.partial(_lru_step_kernel, pos0=pos0),
        grid=(1,),
        in_specs=[
            pl.BlockSpec((b, w), lambda i: (0, 0)),
            pl.BlockSpec((b, w), lambda i: (0, 1)),
            pl.BlockSpec((cwid - 1, b, w), lambda i: (0, 0, 0)),
            pl.BlockSpec((b, w), lambda i: (0, 0)),
            pl.BlockSpec((cwid, w), lambda i: (0, 0)),
            vec, wspec, vec, wspec, vec, vec,
        ],
        out_specs=[
            pl.BlockSpec((b, w), lambda i: (0, 0)),
            pl.BlockSpec((cwid - 1, b, w), lambda i: (0, 0, 0)),
            pl.BlockSpec((b, w), lambda i: (0, 0)),
        ],
        out_shape=[
            jax.ShapeDtypeStruct((b, w), bf16),
            jax.ShapeDtypeStruct((cwid - 1, b, w), f32),
            jax.ShapeDtypeStruct((b, w), f32),
        ],
        compiler_params=_cparams("arbitrary"),
        name="lru_step",
    )(xrg, xrg, buf, h0, conv_w, conv_b, w_a, b_a, w_x, b_x, a_param)


def _merge_kernel(a_ref, b_ref, wpa_ref, wpb_ref, ga_ref, gb_ref, wo_ref, x_ref, o_ref):
    j = pl.program_id(1)

    @pl.when(j == 0)
    def _():
        o_ref[...] = x_ref[...]

    ya = jnp.dot(a_ref[...], wpa_ref[...], preferred_element_type=f32)
    yb = jnp.dot(b_ref[...], wpb_ref[...], preferred_element_type=f32)
    y = ga_ref[...] * ya + gb_ref[...] * yb
    o_ref[...] += jnp.dot(y.astype(bf16), wo_ref[...], preferred_element_type=f32)


def _merge(a, b, w_pa, w_pb, gates, w_o, x, *, bm=512, bn=512):
    m, d = x.shape
    ka = a.shape[1]
    kb = b.shape[1]
    bm = _pick(m, bm)
    bn = _pick(d, bn)
    nj = d // bn
    return pl.pallas_call(
        _merge_kernel,
        grid=(m // bm, nj),
        in_specs=[
            pl.BlockSpec((bm, ka), lambda i, j: (i, 0)),
            pl.BlockSpec((bm, kb), lambda i, j: (i, 0)),
            pl.BlockSpec((ka, bn), lambda i, j: (0, j)),
            pl.BlockSpec((kb, bn), lambda i, j: (0, j)),
            pl.BlockSpec((bm, bn), lambda i, j: (i, j)),
            pl.BlockSpec((bm, bn), lambda i, j: (i, j + nj)),
            pl.BlockSpec((bn, d), lambda i, j: (j, 0)),
            pl.BlockSpec((bm, d), lambda i, j: (i, 0)),
        ],
        out_specs=pl.BlockSpec((bm, d), lambda i, j: (i, 0)),
        out_shape=jax.ShapeDtypeStruct((m, d), f32),
        compiler_params=_cparams("parallel", "arbitrary"),
        name="merge",
    )(a, b, w_pa, w_pb, gates, gates, w_o, x)


def _rope_tables(positions, hd):
    half = hd // 2
    inv = 1.0 / (ROPE_THETA ** (jnp.arange(half, dtype=f32) / half))
    ang = positions.astype(f32)[:, None] * inv[None, :]
    cos, sin = jnp.cos(ang), jnp.sin(ang)
    return jnp.concatenate([cos, cos], axis=-1), jnp.concatenate([-sin, sin], axis=-1)


def _lambda_init(layer):
    return 0.8 - 0.6 * math.exp(-0.3 * layer)


def kernel(x_prompt, x_sample, cache_k, cache_v, state_conv, state_h, page_table, g_ffn1, w_up1, w_down1, g_mix, w_in, g_q, g_k, lam_q1, lam_k1, lam_q2, lam_k2, g_sub, conv_w, conv_b, w_a, b_a, w_x, b_x, a_param, w_pa, w_pb, w_o, g_ffn2, w_up2, w_down2):
    batch, seq, d = x_prompt.shape
    dec_b, dec_t, _ = x_sample.shape
    assert dec_t == 1, "sample group decodes one token per sequence"
    depth, n_pool, page, n_heads, kw = cache_k.shape
    hd = g_q.shape[1]
    vd = g_sub.shape[1]
    assert kw == 2 * hd and vd == 2 * hd
    lw = conv_w.shape[2]
    qk_w = n_heads * 2 * hd
    at_w = n_heads * vd
    past = page_table.shape[1] * page
    scale = hd ** -0.5

    cos_p, sin_p = _rope_tables(jnp.arange(seq), hd)
    cos_s, sin_s = _rope_tables(jnp.full((dec_b,), past), hd)

    yp = x_prompt.reshape(batch * seq, d)
    ys = x_sample.reshape(dec_b, d)
    outs = [[] for _ in range(8)]
    for l in range(depth):
        lam_init = _lambda_init(l)
        wup1, wdn1 = w_up1[l].astype(bf16), w_down1[l].astype(bf16)
        wup2, wdn2 = w_up2[l].astype(bf16), w_down2[l].astype(bf16)
        win = w_in[l].astype(bf16)
        wpa, wpb, wo = w_pa[l].astype(bf16), w_pb[l].astype(bf16), w_o[l].astype(bf16)
        wa, wx = w_a[l].astype(bf16), w_x[l].astype(bf16)
        lam4 = jnp.concatenate([lam_q1[l][None], lam_k1[l][None], lam_q2[l][None], lam_k2[l][None]], 0)
        gsub, gq, gk = g_sub[l][None], g_q[l][None], g_k[l][None]
        lru_w = (conv_w[l], conv_b[l][None], wa, b_a[l][None], wx, b_x[l][None], a_param[l][None])
        ck = cache_k.reshape(depth, n_pool, page * n_heads, kw)
        cv = cache_v.reshape(depth, n_pool, page * n_heads, vd)

        def trunk(x, cos, sin, q_dtype):
            x1, hn = _ffn(x, g_ffn1[l][None], wup1, wdn1, g_mix[l][None])
            (q,) = _proj(hn, win, 0, qk_w, [q_dtype], qk=(gq, cos, sin, scale))
            k32, k16 = _proj(hn, win, qk_w, qk_w, [f32, bf16], qk=(gk, cos, sin, 1.0))
            v32, v16 = _proj(hn, win, 2 * qk_w, at_w, [f32, bf16])
            (xrg,) = _proj(hn, win, 2 * qk_w + at_w, 2 * lw, [f32])
            (gates,) = _proj(hn, win, 2 * qk_w + at_w + 2 * lw, 2 * d, [f32], act="sigmoid")
            return x1, q, k32, k16, v32, v16, xrg, gates

        x1, q, k32, k16, v32, v16, xrg, gates = trunk(yp, cos_p, sin_p, bf16)
        att = _attn_prompt(q, k16, v16, lam4, gsub, batch=batch, seq=seq, n_heads=n_heads,
                           hd=hd, vd=vd, lam_init=lam_init)
        lru, conv_p, h_p = _lru_prompt(xrg, *lru_w, batch=batch, seq=seq, pos0=0)
        x2 = _merge(att, lru, wpa, wpb, gates, wo, x1)
        yp = _ffn(x2, g_ffn2[l][None], wup2, wdn2)
        outs[0].append(k32.reshape(batch, seq, n_heads, kw))
        outs[1].append(v32.reshape(batch, seq, n_heads, vd))
        outs[2].append(conv_p)
        outs[3].append(h_p.reshape(batch, lw))

        x1, q, k32, k16, v32, v16, xrg, gates = trunk(ys, cos_s, sin_s, f32)
        att = _attn_decode(q.reshape(dec_b, n_heads, kw), k32.reshape(dec_b, n_heads, kw),
                           v32.reshape(dec_b, n_heads, vd), ck, cv, page_table, lam4, gsub,
                           layer=l, hd=hd, lam_init=lam_init)
        lru, conv_s, h_s = _lru_step(xrg, jnp.moveaxis(state_conv[l], 1, 0), state_h[l], *lru_w,
                                     pos0=past)
        x2 = _merge(att.reshape(dec_b, at_w), lru, wpa, wpb, gates, wo, x1)
        ys = _ffn(x2, g_ffn2[l][None], wup2, wdn2)
        outs[4].append(k32.reshape(dec_b, dec_t, n_heads, kw))
        outs[5].append(v32.reshape(dec_b, dec_t, n_heads, vd))
        outs[6].append(jnp.moveaxis(conv_s, 0, 1))
        outs[7].append(h_s)

    return (yp.reshape(batch, seq, d), ys.reshape(dec_b, dec_t, d),
            *[jnp.stack(o) for o in outs])
```

```python
import functools
import math

import jax
import jax.numpy as jnp
from jax import lax
from jax.experimental import pallas as pl
from jax.experimental.pallas import tpu as pltpu

EPS = 1e-6
LRU_C = 8.0
ROPE_THETA = 10000.0
LANES = 128
SUBLANES = 8
VMEM_LIMIT_BYTES = 56 * 1024 * 1024

f32 = jnp.float32
bf16 = jnp.bfloat16


def _cparams(*sem):
    return pltpu.CompilerParams(dimension_semantics=sem, vmem_limit_bytes=VMEM_LIMIT_BYTES)


def _rms(x, g):
    return x * lax.rsqrt(jnp.mean(x * x, axis=-1, keepdims=True) + EPS) * g


def _gelu_tanh(x):
    c = math.sqrt(2.0 / math.pi)
    return 0.5 * x * (1.0 + jnp.tanh(c * (x + 0.044715 * (x * x * x))))


def _pick(n, pref):
    if n <= pref:
        return n
    b = pref
    while n % b:
        b //= 2
    return b


def _ffn_kernel(x_ref, g_ref, wg_ref, wu_ref, wd_ref, gn_ref, *rest, nj, emit_norm):
    if emit_norm:
        o_ref, hn_ref, h_ref = rest
    else:
        o_ref, h_ref = rest
    j = pl.program_id(1)

    @pl.when(j == 0)
    def _():
        x = x_ref[...]
        h_ref[...] = _rms(x, g_ref[...]).astype(bf16)
        o_ref[...] = x

    h = h_ref[...]
    gg = jnp.dot(h, wg_ref[...], preferred_element_type=f32)
    uu = jnp.dot(h, wu_ref[...], preferred_element_type=f32)
    act = (gg * jax.nn.sigmoid(gg)) * (uu * 0.5)
    o_ref[...] += jnp.dot(act.astype(bf16), wd_ref[...], preferred_element_type=f32)

    if emit_norm:
        @pl.when(j == nj - 1)
        def _():
            hn_ref[...] = _rms(o_ref[...], gn_ref[...]).astype(bf16)


def _ffn(x, g, w_up, w_down, g_next=None, *, bm=512, bf=512):
    m, d = x.shape
    ff = w_down.shape[0]
    bm = _pick(m, bm)
    bf = _pick(ff, bf)
    nj = ff // bf
    emit_norm = g_next is not None
    gn = g_next if emit_norm else g
    out_shape = [jax.ShapeDtypeStruct((m, d), f32)]
    out_specs = [pl.BlockSpec((bm, d), lambda i, j: (i, 0))]
    if emit_norm:
        out_shape.append(jax.ShapeDtypeStruct((m, d), bf16))
        out_specs.append(pl.BlockSpec((bm, d), lambda i, j: (i, 0)))
    res = pl.pallas_call(
        functools.partial(_ffn_kernel, nj=nj, emit_norm=emit_norm),
        grid=(m // bm, nj),
        in_specs=[
            pl.BlockSpec((bm, d), lambda i, j: (i, 0)),
            pl.BlockSpec((1, d), lambda i, j: (0, 0)),
            pl.BlockSpec((d, bf), lambda i, j: (0, j)),
            pl.BlockSpec((d, bf), lambda i, j: (0, j + nj)),
            pl.BlockSpec((bf, d), lambda i, j: (j, 0)),
            pl.BlockSpec((1, d), lambda i, j: (0, 0)),
        ],
        out_specs=out_specs,
        out_shape=out_shape,
        scratch_shapes=[pltpu.VMEM((bm, d), bf16)],
        compiler_params=_cparams("parallel", "arbitrary"),
        name="ffn",
    )(x, g, w_up, w_up, w_down, gn)
    return res if emit_norm else res[0]


def _proj_qk_kernel(a_ref, w_ref, cos_ref, sin_ref, *o_refs, sub, out_dtypes):
    a = a_ref[...]
    cos, sin = cos_ref[...], sin_ref[...]
    for s0 in range(0, w_ref.shape[1], sub):
        z = jnp.dot(a, w_ref[:, s0:s0 + sub], preferred_element_type=f32)
        for n in range(sub // LANES):
            x = z[:, n * LANES:(n + 1) * LANES]
            inv = lax.rsqrt(jnp.mean(x * x, axis=-1, keepdims=True) + EPS)
            r = (x * cos + pltpu.roll(x, LANES // 2, axis=1) * sin) * inv
            c0 = s0 + n * LANES
            for o_ref, dt in zip(o_refs, out_dtypes):
                o_ref[:, c0:c0 + LANES] = r.astype(dt)


def _proj_plain_kernel(a_ref, w_ref, *o_refs, act, out_dtypes):
    z = jnp.dot(a_ref[...], w_ref[...], preferred_element_type=f32)
    if act == "sigmoid":
        z = jax.nn.sigmoid(z)
    for o_ref, dt in zip(o_refs, out_dtypes):
        o_ref[...] = z.astype(dt)


def _proj(hn, w_in, col0, ncols, out_dtypes, *, qk=None, act=None, bm=1024, bn=1024):
    m, d = hn.shape
    bm = _pick(m, bm)
    if qk is not None:
        bm = _pick(qk[0].shape[0], bm)
    bn = _pick(ncols, bn)
    while col0 % bn:
        bn //= 2
    joff = col0 // bn
    in_specs = [
        pl.BlockSpec((bm, d), lambda i, j: (i, 0)),
        pl.BlockSpec((d, bn), lambda i, j: (0, j + joff)),
    ]
    args = [hn, w_in]
    if qk is not None:
        cos, sin = qk
        npos = cos.shape[0] // bm
        in_specs += [
            pl.BlockSpec((bm, LANES), lambda i, j: (i % npos, 0)),
            pl.BlockSpec((bm, LANES), lambda i, j: (i % npos, 0)),
        ]
        args += [cos, sin]
        body = functools.partial(_proj_qk_kernel, sub=_pick(bn, 2 * LANES), out_dtypes=out_dtypes)
    else:
        body = functools.partial(_proj_plain_kernel, act=act, out_dtypes=out_dtypes)
    return pl.pallas_call(
        body,
        grid=(m // bm, ncols // bn),
        in_specs=in_specs,
        out_specs=[pl.BlockSpec((bm, bn), lambda i, j: (i, j)) for _ in out_dtypes],
        out_shape=[jax.ShapeDtypeStruct((m, ncols), dt) for dt in out_dtypes],
        compiler_params=_cparams("parallel", "arbitrary"),
        name="proj",
    )(*args)


def _lambda(lam_ref, lam_init):
    r = lam_ref[...]
    s1 = jnp.sum(r[0:1] * r[1:2], axis=-1, keepdims=True)
    s2 = jnp.sum(r[2:3] * r[3:4], axis=-1, keepdims=True)
    return jnp.exp(s1) - jnp.exp(s2) + lam_init


def _attn_prompt_kernel(lam_ref, gsub_ref, q_ref, k_ref, v_ref, o_ref,
                        m_ref, l_ref, acc_ref, *, tq, hd, lam_init):
    qi = pl.program_id(2)
    vd = v_ref.shape[1]
    nrep_s = tq // LANES
    nrep_v = vd // LANES

    m_ref[...] = jnp.full(m_ref.shape, -jnp.inf, f32)
    l_ref[...] = jnp.zeros(l_ref.shape, f32)
    acc_ref[...] = jnp.zeros(acc_ref.shape, f32)

    def kv_block(j, masked):
        r0 = pl.multiple_of(j * tq, tq)
        v = v_ref[pl.ds(r0, tq), :]
        for c in range(2):
            q = q_ref[:, c * hd:(c + 1) * hd]
            k = k_ref[pl.ds(r0, tq), c * hd:(c + 1) * hd]
            s = lax.dot_general(q, k, (((1,), (1,)), ((), ())), preferred_element_type=f32)
            if masked:
                row = lax.broadcasted_iota(jnp.int32, s.shape, 0)
                col = lax.broadcasted_iota(jnp.int32, s.shape, 1)
                s = jnp.where(row >= col, s, -jnp.inf)
            m_prev = m_ref[c]
            m_new = jnp.maximum(m_prev, jnp.max(s, axis=1, keepdims=True))
            alpha = jnp.exp2(m_prev - m_new)
            p = jnp.exp2(s - pltpu.repeat(m_new, nrep_s, axis=1))
            l_ref[c] = alpha * l_ref[c] + jnp.sum(p, axis=1, keepdims=True)
            acc_ref[c] = (acc_ref[c] * pltpu.repeat(alpha, nrep_v, axis=1)
                          + jnp.dot(p.astype(bf16), v, preferred_element_type=f32))
            m_ref[c] = m_new

    def body(j, carry):
        kv_block(j, False)
        return carry

    lax.fori_loop(0, qi, body, 0)
    kv_block(qi, True)
    o1 = acc_ref[0] / pltpu.repeat(l_ref[0], nrep_v, axis=1)
    o2 = acc_ref[1] / pltpu.repeat(l_ref[1], nrep_v, axis=1)
    d = o1 - _lambda(lam_ref, lam_init) * o2
    o_ref[...] = (_rms(d, gsub_ref[...]) * (1.0 - lam_init)).astype(o_ref.dtype)


def _attn_prompt(q, k, v, lam4, g_sub, *, batch, seq, n_heads, hd, vd, lam_init, tq=512):
    tq = _pick(seq, tq)
    nq = seq // tq
    return pl.pallas_call(
        functools.partial(_attn_prompt_kernel, tq=tq, hd=hd, lam_init=lam_init),
        grid=(batch, n_heads, nq),
        in_specs=[
            pl.BlockSpec((4, hd), lambda b, h, i: (0, 0)),
            pl.BlockSpec((1, vd), lambda b, h, i: (0, 0)),
            pl.BlockSpec((tq, 2 * hd), lambda b, h, i: (b * nq + i, h)),
            pl.BlockSpec((seq, 2 * hd), lambda b, h, i: (b, h)),
            pl.BlockSpec((seq, vd), lambda b, h, i: (b, h)),
        ],
        out_specs=pl.BlockSpec((tq, vd), lambda b, h, i: (b * nq + i, h)),
        out_shape=jax.ShapeDtypeStruct((batch * seq, n_heads * vd), bf16),
        scratch_shapes=[
            pltpu.VMEM((2, tq, LANES), f32),
            pltpu.VMEM((2, tq, LANES), f32),
            pltpu.VMEM((2, tq, vd), f32),
        ],
        compiler_params=_cparams("parallel", "parallel", "arbitrary"),
        name="attn_prompt",
    )(lam4, g_sub, q, k, v)


def _attn_decode_kernel(pt_ref, lam_ref, gsub_ref, q_ref, kn_ref, vn_ref, *rest, pp, lam_init):
    k_refs = rest[:pp]
    v_refs = rest[pp:2 * pp]
    o_ref, qm_ref, m_ref, l_ref, acc_ref = rest[2 * pp:]
    g = pl.program_id(1)
    ng = pl.num_programs(1)
    nh, w = q_ref.shape[1], q_ref.shape[2]
    hd = w // 2
    rows = k_refs[0].shape[2]
    nrep_w = w // LANES

    @pl.when(g == 0)
    def _():
        q = q_ref[0]
        lane = lax.broadcasted_iota(jnp.int32, (nh, w), 1)
        qm_ref[0:nh, :] = jnp.where(lane < hd, q, 0.0)
        qm_ref[nh:2 * nh, :] = jnp.where(lane >= hd, q, 0.0)
        m_ref[...] = jnp.full(m_ref.shape, -jnp.inf, f32)
        l_ref[...] = jnp.zeros(l_ref.shape, f32)
        acc_ref[...] = jnp.zeros(acc_ref.shape, f32)

    qm = qm_ref[...]
    r_i = lax.broadcasted_iota(jnp.int32, (2 * nh, rows), 0)
    c_i = lax.broadcasted_iota(jnp.int32, (2 * nh, rows), 1)
    valid = (c_i % nh) == (r_i % nh)
    scores = []
    for pi in range(pp):
        s = lax.dot_general(qm, k_refs[pi][0, 0], (((1,), (1,)), ((), ())),
                            preferred_element_type=f32)
        scores.append(jnp.where(valid, s, -jnp.inf))
    mc = scores[0]
    for s in scores[1:]:
        mc = jnp.maximum(mc, s)
    m_prev = m_ref[...]
    m_new = jnp.maximum(m_prev, jnp.max(mc, axis=1, keepdims=True))
    alpha = jnp.exp2(m_prev - m_new)
    m_b = pltpu.repeat(m_new, rows // LANES, axis=1)
    lsum = None
    pv = None
    for pi in range(pp):
        p = jnp.exp2(scores[pi] - m_b)
        ls = jnp.sum(p, axis=1, keepdims=True)
        d = jnp.dot(p, v_refs[pi][0, 0], preferred_element_type=f32)
        lsum = ls if lsum is None else lsum + ls
        pv = d if pv is None else pv + d
    l_new = alpha * l_ref[...] + lsum
    acc_new = pltpu.repeat(alpha, nrep_w, axis=1) * acc_ref[...] + pv
    m_ref[...] = m_new
    l_ref[...] = l_new
    acc_ref[...] = acc_new

    @pl.when(g == ng - 1)
    def _():
        kn = jnp.concatenate([kn_ref[0], kn_ref[0]], axis=0)
        vn = jnp.concatenate([vn_ref[0], vn_ref[0]], axis=0)
        s_self = jnp.sum(qm * kn, axis=1, keepdims=True)
        m_f = jnp.maximum(m_new, s_self)
        a_f = jnp.exp2(m_new - m_f)
        p_f = jnp.exp2(s_self - m_f)
        l_f = a_f * l_new + p_f
        acc_f = (pltpu.repeat(a_f, nrep_w, axis=1) * acc_new
                 + pltpu.repeat(p_f, nrep_w, axis=1) * vn)
        o = acc_f / pltpu.repeat(l_f, nrep_w, axis=1)
        d = o[0:nh] - _lambda(lam_ref, lam_init) * o[nh:2 * nh]
        o_ref[0] = (_rms(d, gsub_ref[...]) * (1.0 - lam_init)).astype(o_ref.dtype)


def _attn_decode(q, k_new, v_new, cache_k, cache_v, page_table, lam4, g_sub, *, layer, hd,
                 lam_init, pp=8):
    b, nh, w = q.shape
    n_pages = page_table.shape[1]
    pp = _pick(n_pages, pp)
    page_rows = cache_k.shape[2]

    def page_spec(pi):
        return pl.BlockSpec((1, 1, page_rows, w),
                            lambda i, g, pt, pi=pi: (layer, pt[i, g * pp + pi], 0, 0))

    row_spec = pl.BlockSpec((1, nh, w), lambda i, g, pt: (i, 0, 0))
    grid_spec = pltpu.PrefetchScalarGridSpec(
        num_scalar_prefetch=1,
        grid=(b, n_pages // pp),
        in_specs=[
            pl.BlockSpec((4, hd), lambda i, g, pt: (0, 0)),
            pl.BlockSpec((1, w), lambda i, g, pt: (0, 0)),
            row_spec, row_spec, row_spec,
        ] + [page_spec(pi) for pi in range(pp)] * 2,
        out_specs=row_spec,
        scratch_shapes=[
            pltpu.VMEM((2 * nh, w), f32),
            pltpu.VMEM((2 * nh, LANES), f32),
            pltpu.VMEM((2 * nh, LANES), f32),
            pltpu.VMEM((2 * nh, w), f32),
        ],
    )
    return pl.pallas_call(
        functools.partial(_attn_decode_kernel, pp=pp, lam_init=lam_init),
        grid_spec=grid_spec,
        out_shape=jax.ShapeDtypeStruct((b, nh, w), bf16),
        compiler_params=_cparams("parallel", "arbitrary"),
        name="attn_decode",
    )(page_table, lam4, g_sub, q, k_new, v_new, *([cache_k] * pp), *([cache_v] * pp))


def _lru_gates(xc, wa_ref, ba_ref, wx_ref, bx_ref, ap_ref):
    nb, blk = wa_ref.shape[0], wa_ref.shape[1]
    xcb = xc.astype(bf16)
    za = jnp.concatenate(
        [jnp.dot(xcb[:, n * blk:(n + 1) * blk], wa_ref[n], preferred_element_type=f32)
         for n in range(nb)], axis=1)
    zx = jnp.concatenate(
        [jnp.dot(xcb[:, n * blk:(n + 1) * blk], wx_ref[n], preferred_element_type=f32)
         for n in range(nb)], axis=1)
    gate_a = jax.nn.sigmoid(za + ba_ref[...])
    gate_x = jax.nn.sigmoid(zx + bx_ref[...])
    ap = -ap_ref[...]
    softplus = jnp.maximum(ap, 0.0) + jnp.log1p(jnp.exp(-jnp.abs(ap)))
    log_a = (-LRU_C) * gate_a * softplus
    a = jnp.exp(log_a)
    mult = jnp.sqrt(-jnp.tanh(log_a) * (a * a + 1.0))
    return a, gate_x, mult


def _lru_prompt_kernel(xr_ref, xg_ref, cw_ref, cb_ref, wa_ref, ba_ref, wx_ref, bx_ref, ap_ref,
                       o_ref, conv_ref, h_ref, xprev_ref, hc_ref, a_s, b_s, *, tt, pos0):
    t = pl.program_id(1)
    nt = pl.num_programs(1)
    w = xr_ref.shape[1]
    cwid = cw_ref.shape[0]
    assert cwid <= SUBLANES + 1
    ng = tt // SUBLANES

    @pl.when(t == 0)
    def _():
        xprev_ref[...] = jnp.zeros(xprev_ref.shape, f32)
        hc_ref[...] = jnp.zeros(hc_ref.shape, f32)

    x = xr_ref[...].reshape(ng, SUBLANES, w)
    prev = xprev_ref[...]
    sub = lax.broadcasted_iota(jnp.int32, (ng, SUBLANES, w), 1)
    xc = x * cw_ref[cwid - 1:cwid, :]
    for i in range(1, cwid):
        r = pltpu.roll(x, i, axis=1)
        r_prev = jnp.concatenate([pltpu.roll(prev, i, axis=0)[None], r[:ng - 1]], axis=0)
        xc = xc + jnp.where(sub >= i, r, r_prev) * cw_ref[cwid - 1 - i:cwid - i, :]
    xc = (xc + cb_ref[...]).reshape(tt, w)
    xprev_ref[...] = x[ng - 1]

    a, gate_x, mult = _lru_gates(xc, wa_ref, ba_ref, wx_ref, bx_ref, ap_ref)
    if pos0 == 0:
        row = lax.broadcasted_iota(jnp.int32, (tt, w), 0)
        mult = jnp.where((row + t * tt) == 0, 1.0, mult)
    b = xc * gate_x * mult

    a = a.reshape(ng, SUBLANES, w)
    b = b.reshape(ng, SUBLANES, w)
    for s in (1, 2, 4):
        keep = sub >= s
        b = jnp.where(keep, a * pltpu.roll(b, s, axis=1) + b, b)
        a = jnp.where(keep, a * pltpu.roll(a, s, axis=1), a)
    a_s[...] = a.reshape(tt, w)
    b_s[...] = b.reshape(tt, w)

    def grp(gi, h):
        r0 = pl.multiple_of(gi * SUBLANES, SUBLANES)
        hg = a_s[pl.ds(r0, SUBLANES), :] * h + b_s[pl.ds(r0, SUBLANES), :]
        b_s[pl.ds(r0, SUBLANES), :] = hg
        return jnp.broadcast_to(hg[SUBLANES - 1:SUBLANES, :], (SUBLANES, w))

    h = lax.fori_loop(0, tt // SUBLANES, grp, hc_ref[...])
    hc_ref[...] = h
    o_ref[...] = (b_s[...] * _gelu_tanh(xg_ref[...])).astype(o_ref.dtype)

    @pl.when(t == nt - 1)
    def _():
        conv_ref[0] = xprev_ref[SUBLANES - (cwid - 1):SUBLANES, :]
        h_ref[0] = h[0:1, :]


def _lru_prompt(xrg, conv_w, conv_b, w_a, b_a, w_x, b_x, a_param, *, batch, seq, pos0, tt=256):
    w = xrg.shape[1] // 2
    tt = _pick(seq, tt)
    nt = seq // tt
    cwid = conv_w.shape[0]
    vec = pl.BlockSpec((1, w), lambda b, t: (0, 0))
    wspec = pl.BlockSpec(w_a.shape, lambda b, t: (0, 0, 0))
    return pl.pallas_call(
        functools.partial(_lru_prompt_kernel, tt=tt, pos0=pos0),
        grid=(batch, nt),
        in_specs=[
            pl.BlockSpec((tt, w), lambda b, t: (b * nt + t, 0)),
            pl.BlockSpec((tt, w), lambda b, t: (b * nt + t, 1)),
            pl.BlockSpec((cwid, w), lambda b, t: (0, 0)),
            vec, wspec, vec, wspec, vec, vec,
        ],
        out_specs=[
            pl.BlockSpec((tt, w), lambda b, t: (b * nt + t, 0)),
            pl.BlockSpec((1, cwid - 1, w), lambda b, t: (b, 0, 0)),
            pl.BlockSpec((1, 1, w), lambda b, t: (b, 0, 0)),
        ],
        out_shape=[
            jax.ShapeDtypeStruct((batch * seq, w), bf16),
            jax.ShapeDtypeStruct((batch, cwid - 1, w), f32),
            jax.ShapeDtypeStruct((batch, 1, w), f32),
        ],
        scratch_shapes=[
            pltpu.VMEM((SUBLANES, w), f32),
            pltpu.VMEM((SUBLANES, w), f32),
            pltpu.VMEM((tt, w), f32),
            pltpu.VMEM((tt, w), f32),
        ],
        compiler_params=_cparams("parallel", "arbitrary"),
        name="lru_prompt",
    )(xrg, xrg, conv_w, conv_b, w_a, b_a, w_x, b_x, a_param)


def _lru_step_kernel(xr_ref, xg_ref, buf_ref, h0_ref, cw_ref, cb_ref, wa_ref, ba_ref, wx_ref,
                     bx_ref, ap_ref, o_ref, conv_ref, h_ref, *, pos0):
    cwid = cw_ref.shape[0]
    x = xr_ref[...]
    xc = cb_ref[...] + x * cw_ref[cwid - 1:cwid, :]
    for i in range(cwid - 1):
        xc = xc + buf_ref[i] * cw_ref[i:i + 1, :]
        if i > 0:
            conv_ref[i - 1] = buf_ref[i]
    conv_ref[cwid - 2] = x
    a, gate_x, mult = _lru_gates(xc, wa_ref, ba_ref, wx_ref, bx_ref, ap_ref)
    if pos0 == 0:
        mult = jnp.ones_like(mult)
    h = a * h0_ref[...] + xc * gate_x * mult
    h_ref[...] = h
    o_ref[...] = (h * _gelu_tanh(xg_ref[...])).astype(o_ref.dtype)


def _lru_step(xrg, buf, h0, conv_w, conv_b, w_a, b_a, w_x, b_x, a_param, *, pos0):
    b = xrg.shape[0]
    w = xrg.shape[1] // 2
    cwid = conv_w.shape[0]
    vec = pl.BlockSpec((1, w), lambda i: (0, 0))
    wspec = pl.BlockSpec(w_a.shape, lambda i: (0, 0, 0))
    return pl.pallas_call(
        functools.partial(_lru_step_kernel, pos0=pos0),
        grid=(1,),
        in_specs=[
            pl.BlockSpec((b, w), lambda i: (0, 0)),
            pl.BlockSpec((b, w), lambda i: (0, 1)),
            pl.BlockSpec((cwid - 1, b, w), lambda i: (0, 0, 0)),
            pl.BlockSpec((b, w), lambda i: (0, 0)),
            pl.BlockSpec((cwid, w), lambda i: (0, 0)),
            vec, wspec, vec, wspec, vec, vec,
        ],
        out_specs=[
            pl.BlockSpec((b, w), lambda i: (0, 0)),
            pl.BlockSpec((cwid - 1, b, w), lambda i: (0, 0, 0)),
            pl.BlockSpec((b, w), lambda i: (0, 0)),
        ],
        out_shape=[
            jax.ShapeDtypeStruct((b, w), bf16),
            jax.ShapeDtypeStruct((cwid - 1, b, w), f32),
            jax.ShapeDtypeStruct((b, w), f32),
        ],
        compiler_params=_cparams("arbitrary"),
        name="lru_step",
    )(xrg, xrg, buf, h0, conv_w, conv_b, w_a, b_a, w_x, b_x, a_param)


def _merge_kernel(a_ref, b_ref, wpa_ref, wpb_ref, ga_ref, gb_ref, wo_ref, x_ref, o_ref):
    j = pl.program_id(1)

    @pl.when(j == 0)
    def _():
        o_ref[...] = x_ref[...]

    ya = jnp.dot(a_ref[...], wpa_ref[...], preferred_element_type=f32)
    yb = jnp.dot(b_ref[...], wpb_ref[...], preferred_element_type=f32)
    y = ga_ref[...] * ya + gb_ref[...] * yb
    o_ref[...] += jnp.dot(y.astype(bf16), wo_ref[...], preferred_element_type=f32)


def _merge(a, b, w_pa, w_pb, gates, w_o, x, *, bm=512, bn=512):
    m, d = x.shape
    ka = a.shape[1]
    kb = b.shape[1]
    bm = _pick(m, bm)
    bn = _pick(d, bn)
    nj = d // bn
    return pl.pallas_call(
        _merge_kernel,
        grid=(m // bm, nj),
        in_specs=[
            pl.BlockSpec((bm, ka), lambda i, j: (i, 0)),
            pl.BlockSpec((bm, kb), lambda i, j: (i, 0)),
            pl.BlockSpec((ka, bn), lambda i, j: (0, j)),
            pl.BlockSpec((kb, bn), lambda i, j: (0, j)),
            pl.BlockSpec((bm, bn), lambda i, j: (i, j)),
            pl.BlockSpec((bm, bn), lambda i, j: (i, j + nj)),
            pl.BlockSpec((bn, d), lambda i, j: (j, 0)),
            pl.BlockSpec((bm, d), lambda i, j: (i, 0)),
        ],
        out_specs=pl.BlockSpec((bm, d), lambda i, j: (i, 0)),
        out_shape=jax.ShapeDtypeStruct((m, d), f32),
        compiler_params=_cparams("parallel", "arbitrary"),
        name="merge",
    )(a, b, w_pa, w_pb, gates, gates, w_o, x)


def _rope_tables(positions, hd):
    half = hd // 2
    inv = 1.0 / (ROPE_THETA ** (jnp.arange(half, dtype=f32) / half))
    ang = positions.astype(f32)[:, None] * inv[None, :]
    cos, sin = jnp.cos(ang), jnp.sin(ang)
    return jnp.concatenate([cos, cos], axis=-1), jnp.concatenate([-sin, sin], axis=-1)


def _lambda_init(layer):
    return 0.8 - 0.6 * math.exp(-0.3 * layer)


def kernel(x_prompt, x_sample, cache_k, cache_v, state_conv, state_h, page_table, g_ffn1, w_up1, w_down1, g_mix, w_in, g_q, g_k, lam_q1, lam_k1, lam_q2, lam_k2, g_sub, conv_w, conv_b, w_a, b_a, w_x, b_x, a_param, w_pa, w_pb, w_o, g_ffn2, w_up2, w_down2):
    batch, seq, d = x_prompt.shape
    dec_b, dec_t, _ = x_sample.shape
    assert dec_t == 1, "sample group decodes one token per sequence"
    depth, n_pool, page, n_heads, kw = cache_k.shape
    hd = g_q.shape[1]
    vd = g_sub.shape[1]
    assert kw == 2 * hd and vd == 2 * hd
    lw = conv_w.shape[2]
    qk_w = n_heads * 2 * hd
    at_w = n_heads * vd
    past = page_table.shape[1] * page
    scale = hd ** -0.5 * math.log2(math.e)

    rope_p = _rope_tables(jnp.arange(seq), hd)
    rope_s = _rope_tables(jnp.full((dec_b,), past), hd)

    def fold(rope, g, s):
        return rope[0] * (g * s), rope[1] * (jnp.roll(g, hd // 2, axis=-1) * s)

    yp = x_prompt.reshape(batch * seq, d)
    ys = x_sample.reshape(dec_b, d)
    outs = [[] for _ in range(8)]
    for l in range(depth):
        lam_init = _lambda_init(l)
        wup1, wdn1 = w_up1[l].astype(bf16), w_down1[l].astype(bf16)
        wup2, wdn2 = w_up2[l].astype(bf16), w_down2[l].astype(bf16)
        win = w_in[l].astype(bf16)
        wpa, wpb, wo = w_pa[l].astype(bf16), w_pb[l].astype(bf16), w_o[l].astype(bf16)
        wa, wx = w_a[l].astype(bf16), w_x[l].astype(bf16)
        lam4 = jnp.concatenate([lam_q1[l][None], lam_k1[l][None], lam_q2[l][None], lam_k2[l][None]], 0)
        gsub, gq, gk = g_sub[l][None], g_q[l][None], g_k[l][None]
        lru_w = (conv_w[l], conv_b[l][None], wa, b_a[l][None], wx, b_x[l][None], a_param[l][None])
        ck = cache_k.reshape(depth, n_pool, page * n_heads, kw)
        cv = cache_v.reshape(depth, n_pool, page * n_heads, vd)

        def trunk(x, rope, q_dtype):
            x1, hn = _ffn(x, g_ffn1[l][None], wup1, wdn1, g_mix[l][None])
            (q,) = _proj(hn, win, 0, qk_w, [q_dtype], qk=fold(rope, gq, scale))
            k32, k16 = _proj(hn, win, qk_w, qk_w, [f32, bf16], qk=fold(rope, gk, 1.0))
            v32, v16 = _proj(hn, win, 2 * qk_w, at_w, [f32, bf16])
            (xrg,) = _proj(hn, win, 2 * qk_w + at_w, 2 * lw, [f32])
            (gates,) = _proj(hn, win, 2 * qk_w + at_w + 2 * lw, 2 * d, [f32], act="sigmoid")
            return x1, q, k32, k16, v32, v16, xrg, gates

        x1, q, k32, k16, v32, v16, xrg, gates = trunk(yp, rope_p, bf16)
        att = _attn_prompt(q, k16, v16, lam4, gsub, batch=batch, seq=seq, n_heads=n_heads,
                           hd=hd, vd=vd, lam_init=lam_init)
        lru, conv_p, h_p = _lru_prompt(xrg, *lru_w, batch=batch, seq=seq, pos0=0)
        x2 = _merge(att, lru, wpa, wpb, gates, wo, x1)
        yp = _ffn(x2, g_ffn2[l][None], wup2, wdn2)
        outs[0].append(k32.reshape(batch, seq, n_heads, kw))
        outs[1].append(v32.reshape(batch, seq, n_heads, vd))
        outs[2].append(conv_p)
        outs[3].append(h_p.reshape(batch, lw))

        x1, q, k32, k16, v32, v16, xrg, gates = trunk(ys, rope_s, f32)
        att = _attn_decode(q.reshape(dec_b, n_heads, kw), k32.reshape(dec_b, n_heads, kw),
                           v32.reshape(dec_b, n_heads, vd), ck, cv, page_table, lam4, gsub,
                           layer=l, hd=hd, lam_init=lam_init)
        lru, conv_s, h_s = _lru_step(xrg, jnp.moveaxis(state_conv[l], 1, 0), state_h[l], *lru_w,
                                     pos0=past)
        x2 = _merge(att.reshape(dec_b, at_w), lru, wpa, wpb, gates, wo, x1)
        ys = _ffn(x2, g_ffn2[l][None], wup2, wdn2)
        outs[4].append(k32.reshape(dec_b, dec_t, n_heads, kw))
        outs[5].append(v32.reshape(dec_b, dec_t, n_heads, vd))
        outs[6].append(jnp.moveaxis(conv_s, 0, 1))
        outs[7].append(h_s)

    return (yp.reshape(batch, seq, d), ys.reshape(dec_b, dec_t, d),
            *[jnp.stack(o) for o in outs])
```

```python
import functools
import math

import jax
import jax.numpy as jnp
from jax import lax
from jax.experimental import pallas as pl
from jax.experimental.pallas import tpu as pltpu

EPS = 1e-6
LRU_C = 8.0
ROPE_THETA = 10000.0
LANES = 128
SUBLANES = 8
VMEM_LIMIT_BYTES = 56 * 1024 * 1024
FFN_VMEM_LIMIT_BYTES = 62 * 1024 * 1024

f32 = jnp.float32
bf16 = jnp.bfloat16


def _cparams(*sem):
    return pltpu.CompilerParams(dimension_semantics=sem, vmem_limit_bytes=VMEM_LIMIT_BYTES)


def _rms(x, g):
    return x * lax.rsqrt(jnp.mean(x * x, axis=-1, keepdims=True) + EPS) * g


def _lane_repeat(x, n):
    return x if n == 1 else jnp.concatenate([x] * n, axis=1)


def _gelu_tanh(x):
    c = math.sqrt(2.0 / math.pi)
    return 0.5 * x * (1.0 + jnp.tanh(c * (x + 0.044715 * (x * x * x))))


def _pick(n, pref):
    if n <= pref:
        return n
    b = pref
    while n % b:
        b //= 2
    return b


def _ffn_kernel(x_ref, g_ref, wg_ref, wu_ref, wd_ref, gn_ref, *rest, nj, emit_norm):
    if emit_norm:
        o_ref, hn_ref, h_ref = rest
    else:
        o_ref, h_ref = rest
    j = pl.program_id(1)

    @pl.when(j == 0)
    def _():
        x = x_ref[...]
        h_ref[...] = _rms(x, g_ref[...]).astype(bf16)
        o_ref[...] = x

    h = h_ref[...]
    gg = jnp.dot(h, wg_ref[...], preferred_element_type=f32)
    uu = jnp.dot(h, wu_ref[...], preferred_element_type=f32)
    act = (gg * jax.nn.sigmoid(gg)) * (uu * 0.5)
    o_ref[...] += jnp.dot(act.astype(bf16), wd_ref[...], preferred_element_type=f32)

    if emit_norm:
        @pl.when(j == nj - 1)
        def _():
            hn_ref[...] = _rms(o_ref[...], gn_ref[...]).astype(bf16)


def _ffn(x, g, w_up, w_down, g_next=None, *, bm=1024, bf=512):
    m, d = x.shape
    ff = w_down.shape[0]
    bm = _pick(m, bm)
    bf = _pick(ff, bf)
    nj = ff // bf
    emit_norm = g_next is not None
    gn = g_next if emit_norm else g
    out_shape = [jax.ShapeDtypeStruct((m, d), f32)]
    out_specs = [pl.BlockSpec((bm, d), lambda i, j: (i, 0))]
    if emit_norm:
        out_shape.append(jax.ShapeDtypeStruct((m, d), bf16))
        out_specs.append(pl.BlockSpec((bm, d), lambda i, j: (i, 0)))
    res = pl.pallas_call(
        functools.partial(_ffn_kernel, nj=nj, emit_norm=emit_norm),
        grid=(m // bm, nj),
        in_specs=[
            pl.BlockSpec((bm, d), lambda i, j: (i, 0), pipeline_mode=pl.Buffered(1)),
            pl.BlockSpec((1, d), lambda i, j: (0, 0)),
            pl.BlockSpec((d, bf), lambda i, j: (0, j)),
            pl.BlockSpec((d, bf), lambda i, j: (0, j + nj)),
            pl.BlockSpec((bf, d), lambda i, j: (j, 0)),
            pl.BlockSpec((1, d), lambda i, j: (0, 0)),
        ],
        out_specs=out_specs,
        out_shape=out_shape,
        scratch_shapes=[pltpu.VMEM((bm, d), bf16)],
        compiler_params=pltpu.CompilerParams(dimension_semantics=("parallel", "arbitrary"),
                                             vmem_limit_bytes=FFN_VMEM_LIMIT_BYTES),
        name="ffn",
    )(x, g, w_up, w_up, w_down, gn)
    return res if emit_norm else res[0]


def _proj_qk_kernel(a_ref, w_ref, cos_ref, sin_ref, *o_refs, sub, out_dtypes):
    a = a_ref[...]
    cos, sin = cos_ref[...], sin_ref[...]
    for s0 in range(0, w_ref.shape[1], sub):
        z = jnp.dot(a, w_ref[:, s0:s0 + sub], preferred_element_type=f32)
        for n in range(sub // LANES):
            x = z[:, n * LANES:(n + 1) * LANES]
            inv = lax.rsqrt(jnp.mean(x * x, axis=-1, keepdims=True) + EPS)
            r = (x * cos + pltpu.roll(x, LANES // 2, axis=1) * sin) * inv
            c0 = s0 + n * LANES
            for o_ref, dt in zip(o_refs, out_dtypes):
                o_ref[:, c0:c0 + LANES] = r.astype(dt)


def _proj_plain_kernel(a_ref, w_ref, *o_refs, act, out_dtypes):
    z = jnp.dot(a_ref[...], w_ref[...], preferred_element_type=f32)
    if act == "sigmoid":
        z = jax.nn.sigmoid(z)
    for o_ref, dt in zip(o_refs, out_dtypes):
        o_ref[...] = z.astype(dt)


def _proj(hn, w_in, col0, ncols, out_dtypes, *, qk=None, act=None, bm=1024, bn=1024):
    m, d = hn.shape
    if qk is None and len(out_dtypes) == 1:
        bm = 2 * bm
    bm = _pick(m, bm)
    if qk is not None:
        bm = _pick(qk[0].shape[0], bm)
    bn = _pick(ncols, bn)
    while col0 % bn:
        bn //= 2
    joff = col0 // bn
    in_specs = [
        pl.BlockSpec((bm, d), lambda i, j: (i, 0)),
        pl.BlockSpec((d, bn), lambda i, j: (0, j + joff)),
    ]
    args = [hn, w_in]
    if qk is not None:
        cos, sin = qk
        npos = cos.shape[0] // bm
        in_specs += [
            pl.BlockSpec((bm, LANES), lambda i, j: (i % npos, 0)),
            pl.BlockSpec((bm, LANES), lambda i, j: (i % npos, 0)),
        ]
        args += [cos, sin]
        body = functools.partial(_proj_qk_kernel, sub=_pick(bn, 2 * LANES), out_dtypes=out_dtypes)
    else:
        body = functools.partial(_proj_plain_kernel, act=act, out_dtypes=out_dtypes)
    return pl.pallas_call(
        body,
        grid=(m // bm, ncols // bn),
        in_specs=in_specs,
        out_specs=[pl.BlockSpec((bm, bn), lambda i, j: (i, j)) for _ in out_dtypes],
        out_shape=[jax.ShapeDtypeStruct((m, ncols), dt) for dt in out_dtypes],
        compiler_params=_cparams("parallel", "arbitrary"),
        name="proj",
    )(*args)


def _lambda(lam_ref, lam_init):
    r = lam_ref[...]
    s1 = jnp.sum(r[0:1] * r[1:2], axis=-1, keepdims=True)
    s2 = jnp.sum(r[2:3] * r[3:4], axis=-1, keepdims=True)
    return jnp.exp(s1) - jnp.exp(s2) + lam_init


def _attn_prompt_kernel(lam_ref, gsub_ref, q_ref, k_ref, v_ref, o_ref,
                        m_ref, l_ref, acc_ref, *, tk, hd, lam_init):
    qi = pl.program_id(2)
    vd = v_ref.shape[1]
    nparts = q_ref.shape[0] // tk
    nrep_s = tk // LANES
    nrep_v = vd // LANES

    m_ref[...] = jnp.full(m_ref.shape, -jnp.inf, f32)
    l_ref[...] = jnp.zeros(l_ref.shape, f32)
    acc_ref[...] = jnp.zeros(acc_ref.shape, f32)

    def kv_block(j, parts):
        r0 = pl.multiple_of(j * tk, tk)
        v = v_ref[pl.ds(r0, tk), :]
        for c in range(2):
            k = k_ref[pl.ds(r0, tk), c * hd:(c + 1) * hd]
            for part, diagonal in parts:
                rows = slice(part * tk, (part + 1) * tk)
                q = q_ref[rows, c * hd:(c + 1) * hd]
                s = lax.dot_general(q, k, (((1,), (1,)), ((), ())), preferred_element_type=f32)
                if diagonal:
                    row = lax.broadcasted_iota(jnp.int32, s.shape, 0)
                    col = lax.broadcasted_iota(jnp.int32, s.shape, 1)
                    s = jnp.where(row >= col, s, -jnp.inf)
                m_prev = m_ref[c, rows]
                m_new = jnp.maximum(m_prev, jnp.max(s, axis=1, keepdims=True))
                alpha = jnp.exp2(m_prev - m_new)
                p = jnp.exp2(s - _lane_repeat(m_new, nrep_s))
                l_ref[c, rows] = alpha * l_ref[c, rows] + jnp.sum(p, axis=1, keepdims=True)
                acc_ref[c, rows] = (acc_ref[c, rows] * _lane_repeat(alpha, nrep_v)
                                    + jnp.dot(p.astype(bf16), v, preferred_element_type=f32))
                m_ref[c, rows] = m_new

    def body(j, carry):
        kv_block(j, [(part, False) for part in range(nparts)])
        return carry

    lax.fori_loop(0, nparts * qi, body, 0)
    for d in range(nparts):
        kv_block(nparts * qi + d, [(part, part == d) for part in range(d, nparts)])
    o1 = acc_ref[0] / _lane_repeat(l_ref[0], nrep_v)
    o2 = acc_ref[1] / _lane_repeat(l_ref[1], nrep_v)
    d = o1 - _lambda(lam_ref, lam_init) * o2
    o_ref[...] = (_rms(d, gsub_ref[...]) * (1.0 - lam_init)).astype(o_ref.dtype)


def _attn_prompt(q, k, v, lam4, g_sub, *, batch, seq, n_heads, hd, vd, lam_init, tq=1024, tk=512):
    tq = _pick(seq, tq)
    tk = _pick(tq, tk)
    nq = seq // tq
    return pl.pallas_call(
        functools.partial(_attn_prompt_kernel, tk=tk, hd=hd, lam_init=lam_init),
        grid=(batch, n_heads, nq),
        in_specs=[
            pl.BlockSpec((4, hd), lambda b, h, i: (0, 0)),
            pl.BlockSpec((1, vd), lambda b, h, i: (0, 0)),
            pl.BlockSpec((tq, 2 * hd), lambda b, h, i: (b * nq + i, h)),
            pl.BlockSpec((seq, 2 * hd), lambda b, h, i: (b, h)),
            pl.BlockSpec((seq, vd), lambda b, h, i: (b, h)),
        ],
        out_specs=pl.BlockSpec((tq, vd), lambda b, h, i: (b * nq + i, h)),
        out_shape=jax.ShapeDtypeStruct((batch * seq, n_heads * vd), bf16),
        scratch_shapes=[
            pltpu.VMEM((2, tq, LANES), f32),
            pltpu.VMEM((2, tq, LANES), f32),
            pltpu.VMEM((2, tq, vd), f32),
        ],
        compiler_params=_cparams("parallel", "parallel", "arbitrary"),
        name="attn_prompt",
    )(lam4, g_sub, q, k, v)


def _attn_decode_kernel(pt_ref, lam_ref, gsub_ref, q_ref, kn_ref, vn_ref, *rest, pp, lam_init):
    k_refs = rest[:pp]
    v_refs = rest[pp:2 * pp]
    o_ref, qm_ref, m_ref, l_ref, acc_ref = rest[2 * pp:]
    g = pl.program_id(1)
    ng = pl.num_programs(1)
    nh, w = q_ref.shape[1], q_ref.shape[2]
    hd = w // 2
    rows = k_refs[0].shape[2]
    nrep_w = w // LANES

    @pl.when(g == 0)
    def _():
        q = q_ref[0]
        lane = lax.broadcasted_iota(jnp.int32, (nh, w), 1)
        qm_ref[0:nh, :] = jnp.where(lane < hd, q, 0.0)
        qm_ref[nh:2 * nh, :] = jnp.where(lane >= hd, q, 0.0)
        m_ref[...] = jnp.full(m_ref.shape, -jnp.inf, f32)
        l_ref[...] = jnp.zeros(l_ref.shape, f32)
        acc_ref[...] = jnp.zeros(acc_ref.shape, f32)

    qm = qm_ref[...]
    r_i = lax.broadcasted_iota(jnp.int32, (2 * nh, rows), 0)
    c_i = lax.broadcasted_iota(jnp.int32, (2 * nh, rows), 1)
    valid = (c_i % nh) == (r_i % nh)
    scores = []
    for pi in range(pp):
        s = lax.dot_general(qm, k_refs[pi][0, 0], (((1,), (1,)), ((), ())),
                            preferred_element_type=f32)
        scores.append(jnp.where(valid, s, -jnp.inf))
    mc = scores[0]
    for s in scores[1:]:
        mc = jnp.maximum(mc, s)
    m_prev = m_ref[...]
    m_new = jnp.maximum(m_prev, jnp.max(mc, axis=1, keepdims=True))
    alpha = jnp.exp2(m_prev - m_new)
    m_b = _lane_repeat(m_new, rows // LANES)
    lsum = None
    pv = None
    for pi in range(pp):
        p = jnp.exp2(scores[pi] - m_b)
        ls = jnp.sum(p, axis=1, keepdims=True)
        d = jnp.dot(p, v_refs[pi][0, 0], preferred_element_type=f32)
        lsum = ls if lsum is None else lsum + ls
        pv = d if pv is None else pv + d
    l_new = alpha * l_ref[...] + lsum
    acc_new = _lane_repeat(alpha, nrep_w) * acc_ref[...] + pv
    m_ref[...] = m_new
    l_ref[...] = l_new
    acc_ref[...] = acc_new

    @pl.when(g == ng - 1)
    def _():
        kn = jnp.concatenate([kn_ref[0], kn_ref[0]], axis=0)
        vn = jnp.concatenate([vn_ref[0], vn_ref[0]], axis=0)
        s_self = jnp.sum(qm * kn, axis=1, keepdims=True)
        m_f = jnp.maximum(m_new, s_self)
        a_f = jnp.exp2(m_new - m_f)
        p_f = jnp.exp2(s_self - m_f)
        l_f = a_f * l_new + p_f
        acc_f = _lane_repeat(a_f, nrep_w) * acc_new + _lane_repeat(p_f, nrep_w) * vn
        o = acc_f / _lane_repeat(l_f, nrep_w)
        d = o[0:nh] - _lambda(lam_ref, lam_init) * o[nh:2 * nh]
        o_ref[0] = (_rms(d, gsub_ref[...]) * (1.0 - lam_init)).astype(o_ref.dtype)


def _attn_decode(q, k_new, v_new, cache_k, cache_v, page_table, lam4, g_sub, *, layer, hd,
                 lam_init, pp=8):
    b, nh, w = q.shape
    n_pages = page_table.shape[1]
    pp = _pick(n_pages, pp)
    page_rows = cache_k.shape[2]

    def page_spec(pi):
        return pl.BlockSpec((1, 1, page_rows, w),
                            lambda i, g, pt, pi=pi: (layer, pt[i, g * pp + pi], 0, 0))

    row_spec = pl.BlockSpec((1, nh, w), lambda i, g, pt: (i, 0, 0))
    grid_spec = pltpu.PrefetchScalarGridSpec(
        num_scalar_prefetch=1,
        grid=(b, n_pages // pp),
        in_specs=[
            pl.BlockSpec((4, hd), lambda i, g, pt: (0, 0)),
            pl.BlockSpec((1, w), lambda i, g, pt: (0, 0)),
            row_spec, row_spec, row_spec,
        ] + [page_spec(pi) for pi in range(pp)] * 2,
        out_specs=row_spec,
        scratch_shapes=[
            pltpu.VMEM((2 * nh, w), f32),
            pltpu.VMEM((2 * nh, LANES), f32),
            pltpu.VMEM((2 * nh, LANES), f32),
            pltpu.VMEM((2 * nh, w), f32),
        ],
    )
    return pl.pallas_call(
        functools.partial(_attn_decode_kernel, pp=pp, lam_init=lam_init),
        grid_spec=grid_spec,
        out_shape=jax.ShapeDtypeStruct((b, nh, w), bf16),
        compiler_params=_cparams("parallel", "arbitrary"),
        name="attn_decode",
    )(page_table, lam4, g_sub, q, k_new, v_new, *([cache_k] * pp), *([cache_v] * pp))


def _lru_gates(xc, wa_ref, ba_ref, wx_ref, bx_ref, ap_ref):
    nb, blk = wa_ref.shape[0], wa_ref.shape[1]
    xcb = xc.astype(bf16)
    za = jnp.concatenate(
        [jnp.dot(xcb[:, n * blk:(n + 1) * blk], wa_ref[n], preferred_element_type=f32)
         for n in range(nb)], axis=1)
    zx = jnp.concatenate(
        [jnp.dot(xcb[:, n * blk:(n + 1) * blk], wx_ref[n], preferred_element_type=f32)
         for n in range(nb)], axis=1)
    gate_a = jax.nn.sigmoid(za + ba_ref[...])
    gate_x = jax.nn.sigmoid(zx + bx_ref[...])
    ap = -ap_ref[...]
    softplus = jnp.maximum(ap, 0.0) + jnp.log1p(jnp.exp(-jnp.abs(ap)))
    log_a = (-LRU_C) * gate_a * softplus
    a = jnp.exp(log_a)
    mult = jnp.sqrt(-jnp.tanh(log_a) * (a * a + 1.0))
    return a, gate_x, mult


def _lru_prompt_kernel(xr_ref, xg_ref, cw_ref, cb_ref, wa_ref, ba_ref, wx_ref, bx_ref, ap_ref,
                       o_ref, conv_ref, h_ref, xprev_ref, hc_ref, a_s, b_s, *, tt, pos0):
    t = pl.program_id(1)
    nt = pl.num_programs(1)
    w = xr_ref.shape[1]
    cwid = cw_ref.shape[0]
    assert cwid <= SUBLANES + 1
    ng = tt // SUBLANES

    @pl.when(t == 0)
    def _():
        xprev_ref[...] = jnp.zeros(xprev_ref.shape, f32)
        hc_ref[...] = jnp.zeros(hc_ref.shape, f32)

    x = xr_ref[...].reshape(ng, SUBLANES, w)
    prev = xprev_ref[...]
    sub = lax.broadcasted_iota(jnp.int32, (ng, SUBLANES, w), 1)
    xc = x * cw_ref[cwid - 1:cwid, :]
    for i in range(1, cwid):
        r = pltpu.roll(x, i, axis=1)
        r_prev = jnp.concatenate([pltpu.roll(prev, i, axis=0)[None], r[:ng - 1]], axis=0)
        xc = xc + jnp.where(sub >= i, r, r_prev) * cw_ref[cwid - 1 - i:cwid - i, :]
    xc = (xc + cb_ref[...]).reshape(tt, w)
    xprev_ref[...] = x[ng - 1]

    a, gate_x, mult = _lru_gates(xc, wa_ref, ba_ref, wx_ref, bx_ref, ap_ref)
    if pos0 == 0:
        row = lax.broadcasted_iota(jnp.int32, (tt, w), 0)
        mult = jnp.where((row + t * tt) == 0, 1.0, mult)
    b = xc * gate_x * mult

    a = a.reshape(ng, SUBLANES, w)
    b = b.reshape(ng, SUBLANES, w)
    for s in (1, 2, 4):
        keep = sub >= s
        b = jnp.where(keep, a * pltpu.roll(b, s, axis=1) + b, b)
        a = jnp.where(keep, a * pltpu.roll(a, s, axis=1), a)
    a_s[...] = a.reshape(tt, w)
    b_s[...] = b.reshape(tt, w)

    def grp(gi, h):
        r0 = pl.multiple_of(gi * SUBLANES, SUBLANES)
        hg = a_s[pl.ds(r0, SUBLANES), :] * h + b_s[pl.ds(r0, SUBLANES), :]
        b_s[pl.ds(r0, SUBLANES), :] = hg
        return jnp.broadcast_to(hg[SUBLANES - 1:SUBLANES, :], (SUBLANES, w))

    h = lax.fori_loop(0, tt // SUBLANES, grp, hc_ref[...])
    hc_ref[...] = h
    o_ref[...] = (b_s[...] * _gelu_tanh(xg_ref[...].astype(f32))).astype(o_ref.dtype)

    @pl.when(t == nt - 1)
    def _():
        conv_ref[0] = xprev_ref[SUBLANES - (cwid - 1):SUBLANES, :]
        h_ref[0] = h[0:1, :]


def _lru_prompt(xr, xg, conv_w, conv_b, w_a, b_a, w_x, b_x, a_param, *, batch, seq, pos0, tt=256):
    w = xr.shape[1]
    tt = _pick(seq, tt)
    nt = seq // tt
    cwid = conv_w.shape[0]
    vec = pl.BlockSpec((1, w), lambda b, t: (0, 0))
    wspec = pl.BlockSpec(w_a.shape, lambda b, t: (0, 0, 0))
    return pl.pallas_call(
        functools.partial(_lru_prompt_kernel, tt=tt, pos0=pos0),
        grid=(batch, nt),
        in_specs=[
            pl.BlockSpec((tt, w), lambda b, t: (b * nt + t, 0)),
            pl.BlockSpec((tt, w), lambda b, t: (b * nt + t, 0)),
            pl.BlockSpec((cwid, w), lambda b, t: (0, 0)),
            vec, wspec, vec, wspec, vec, vec,
        ],
        out_specs=[
            pl.BlockSpec((tt, w), lambda b, t: (b * nt + t, 0)),
            pl.BlockSpec((1, cwid - 1, w), lambda b, t: (b, 0, 0)),
            pl.BlockSpec((1, 1, w), lambda b, t: (b, 0, 0)),
        ],
        out_shape=[
            jax.ShapeDtypeStruct((batch * seq, w), bf16),
            jax.ShapeDtypeStruct((batch, cwid - 1, w), f32),
            jax.ShapeDtypeStruct((batch, 1, w), f32),
        ],
        scratch_shapes=[
            pltpu.VMEM((SUBLANES, w), f32),
            pltpu.VMEM((SUBLANES, w), f32),
            pltpu.VMEM((tt, w), f32),
            pltpu.VMEM((tt, w), f32),
        ],
        compiler_params=_cparams("parallel", "arbitrary"),
        name="lru_prompt",
    )(xr, xg, conv_w, conv_b, w_a, b_a, w_x, b_x, a_param)


def _lru_step_kernel(xr_ref, xg_ref, buf_ref, h0_ref, cw_ref, cb_ref, wa_ref, ba_ref, wx_ref,
                     bx_ref, ap_ref, o_ref, conv_ref, h_ref, *, pos0):
    cwid = cw_ref.shape[0]
    x = xr_ref[...]
    xc = cb_ref[...] + x * cw_ref[cwid - 1:cwid, :]
    for i in range(cwid - 1):
        xc = xc + buf_ref[i] * cw_ref[i:i + 1, :]
        if i > 0:
            conv_ref[i - 1] = buf_ref[i]
    conv_ref[cwid - 2] = x
    a, gate_x, mult = _lru_gates(xc, wa_ref, ba_ref, wx_ref, bx_ref, ap_ref)
    if pos0 == 0:
        mult = jnp.ones_like(mult)
    h = a * h0_ref[...] + xc * gate_x * mult
    h_ref[...] = h
    o_ref[...] = (h * _gelu_tanh(xg_ref[...].astype(f32))).astype(o_ref.dtype)


def _lru_step(xr, xg, buf, h0, conv_w, conv_b, w_a, b_a, w_x, b_x, a_param, *, pos0):
    b, w = xr.shape
    cwid = conv_w.shape[0]
    vec = pl.BlockSpec((1, w), lambda i: (0, 0))
    wspec = pl.BlockSpec(w_a.shape, lambda i: (0, 0, 0))
    return pl.pallas_call(
        functools.partial(_lru_step_kernel, pos0=pos0),
        grid=(1,),
        in_specs=[
            pl.BlockSpec((b, w), lambda i: (0, 0)),
            pl.BlockSpec((b, w), lambda i: (0, 0)),
            pl.BlockSpec((cwid - 1, b, w), lambda i: (0, 0, 0)),
            pl.BlockSpec((b, w), lambda i: (0, 0)),
            pl.BlockSpec((cwid, w), lambda i: (0, 0)),
            vec, wspec, vec, wspec, vec, vec,
        ],
        out_specs=[
            pl.BlockSpec((b, w), lambda i: (0, 0)),
            pl.BlockSpec((cwid - 1, b, w), lambda i: (0, 0, 0)),
            pl.BlockSpec((b, w), lambda i: (0, 0)),
        ],
        out_shape=[
            jax.ShapeDtypeStruct((b, w), bf16),
            jax.ShapeDtypeStruct((cwid - 1, b, w), f32),
            jax.ShapeDtypeStruct((b, w), f32),
        ],
        compiler_params=_cparams("arbitrary"),
        name="lru_step",
    )(xr, xg, buf, h0, conv_w, conv_b, w_a, b_a, w_x, b_x, a_param)


def _merge_kernel(a_ref, b_ref, wpa_ref, wpb_ref, ga_ref, gb_ref, wo_ref, x_ref, o_ref):
    j = pl.program_id(1)

    @pl.when(j == 0)
    def _():
        o_ref[...] = x_ref[...]

    ya = jnp.dot(a_ref[...], wpa_ref[...], preferred_element_type=f32)
    yb = jnp.dot(b_ref[...], wpb_ref[...], preferred_element_type=f32)
    y = ga_ref[...].astype(f32) * ya + gb_ref[...].astype(f32) * yb
    o_ref[...] += jnp.dot(y.astype(bf16), wo_ref[...], preferred_element_type=f32)


def _merge(a, b, w_pa, w_pb, gates, w_o, x, *, bm=1024, bn=256):
    m, d = x.shape
    ka = a.shape[1]
    kb = b.shape[1]
    bm = _pick(m, bm)
    bn = _pick(d, bn)
    nj = d // bn
    return pl.pallas_call(
        _merge_kernel,
        grid=(m // bm, nj),
        in_specs=[
            pl.BlockSpec((bm, ka), lambda i, j: (i, 0)),
            pl.BlockSpec((bm, kb), lambda i, j: (i, 0)),
            pl.BlockSpec((ka, bn), lambda i, j: (0, j)),
            pl.BlockSpec((kb, bn), lambda i, j: (0, j)),
            pl.BlockSpec((bm, bn), lambda i, j: (i, j)),
            pl.BlockSpec((bm, bn), lambda i, j: (i, j + nj)),
            pl.BlockSpec((bn, d), lambda i, j: (j, 0)),
            pl.BlockSpec((bm, d), lambda i, j: (i, 0), pipeline_mode=pl.Buffered(1)),
        ],
        out_specs=pl.BlockSpec((bm, d), lambda i, j: (i, 0)),
        out_shape=jax.ShapeDtypeStruct((m, d), f32),
        compiler_params=_cparams("parallel", "arbitrary"),
        name="merge",
    )(a, b, w_pa, w_pb, gates, gates, w_o, x)


def _rope_tables(positions, hd):
    half = hd // 2
    inv = 1.0 / (ROPE_THETA ** (jnp.arange(half, dtype=f32) / half))
    ang = positions.astype(f32)[:, None] * inv[None, :]
    cos, sin = jnp.cos(ang), jnp.sin(ang)
    return jnp.concatenate([cos, cos], axis=-1), jnp.concatenate([-sin, sin], axis=-1)


def _lambda_init(layer):
    return 0.8 - 0.6 * math.exp(-0.3 * layer)


def kernel(x_prompt, x_sample, cache_k, cache_v, state_conv, state_h, page_table, g_ffn1, w_up1, w_down1, g_mix, w_in, g_q, g_k, lam_q1, lam_k1, lam_q2, lam_k2, g_sub, conv_w, conv_b, w_a, b_a, w_x, b_x, a_param, w_pa, w_pb, w_o, g_ffn2, w_up2, w_down2):
    batch, seq, d = x_prompt.shape
    dec_b, dec_t, _ = x_sample.shape
    assert dec_t == 1, "sample group decodes one token per sequence"
    depth, n_pool, page, n_heads, kw = cache_k.shape
    hd = g_q.shape[1]
    vd = g_sub.shape[1]
    assert kw == 2 * hd and vd == 2 * hd
    lw = conv_w.shape[2]
    qk_w = n_heads * 2 * hd
    at_w = n_heads * vd
    past = page_table.shape[1] * page
    scale = hd ** -0.5 * math.log2(math.e)

    rope_p = _rope_tables(jnp.arange(seq), hd)
    rope_s = _rope_tables(jnp.full((dec_b,), past), hd)

    def fold(rope, g, s):
        return rope[0] * (g * s), rope[1] * (jnp.roll(g, hd // 2, axis=-1) * s)

    yp = x_prompt.reshape(batch * seq, d)
    ys = x_sample.reshape(dec_b, d)
    outs = [[] for _ in range(8)]
    for l in range(depth):
        lam_init = _lambda_init(l)
        wup1, wdn1 = w_up1[l].astype(bf16), w_down1[l].astype(bf16)
        wup2, wdn2 = w_up2[l].astype(bf16), w_down2[l].astype(bf16)
        win = w_in[l].astype(bf16)
        wpa, wpb, wo = w_pa[l].astype(bf16), w_pb[l].astype(bf16), w_o[l].astype(bf16)
        wa, wx = w_a[l].astype(bf16), w_x[l].astype(bf16)
        lam4 = jnp.concatenate([lam_q1[l][None], lam_k1[l][None], lam_q2[l][None], lam_k2[l][None]], 0)
        gsub, gq, gk = g_sub[l][None], g_q[l][None], g_k[l][None]
        lru_w = (conv_w[l], conv_b[l][None], wa, b_a[l][None], wx, b_x[l][None], a_param[l][None])
        ck = cache_k.reshape(depth, n_pool, page * n_heads, kw)
        cv = cache_v.reshape(depth, n_pool, page * n_heads, vd)

        def trunk(x, rope, q_dtype):
            x1, hn = _ffn(x, g_ffn1[l][None], wup1, wdn1, g_mix[l][None])
            (q,) = _proj(hn, win, 0, qk_w, [q_dtype], qk=fold(rope, gq, scale))
            k32, k16 = _proj(hn, win, qk_w, qk_w, [f32, bf16], qk=fold(rope, gk, 1.0))
            v32, v16 = _proj(hn, win, 2 * qk_w, at_w, [f32, bf16])
            (xr,) = _proj(hn, win, 2 * qk_w + at_w, lw, [f32])
            (xg,) = _proj(hn, win, 2 * qk_w + at_w + lw, lw, [bf16])
            (gates,) = _proj(hn, win, 2 * qk_w + at_w + 2 * lw, 2 * d, [bf16], act="sigmoid")
            return x1, q, k32, k16, v32, v16, xr, xg, gates

        x1, q, k32, k16, v32, v16, xr, xg, gates = trunk(yp, rope_p, bf16)
        att = _attn_prompt(q, k16, v16, lam4, gsub, batch=batch, seq=seq, n_heads=n_heads,
                           hd=hd, vd=vd, lam_init=lam_init)
        lru, conv_p, h_p = _lru_prompt(xr, xg, *lru_w, batch=batch, seq=seq, pos0=0)
        x2 = _merge(att, lru, wpa, wpb, gates, wo, x1)
        yp = _ffn(x2, g_ffn2[l][None], wup2, wdn2)
        outs[0].append(k32.reshape(batch, seq, n_heads, kw))
        outs[1].append(v32.reshape(batch, seq, n_heads, vd))
        outs[2].append(conv_p)
        outs[3].append(h_p.reshape(batch, lw))

        x1, q, k32, k16, v32, v16, xr, xg, gates = trunk(ys, rope_s, f32)
        att = _attn_decode(q.reshape(dec_b, n_heads, kw), k32.reshape(dec_b, n_heads, kw),
                           v32.reshape(dec_b, n_heads, vd), ck, cv, page_table, lam4, gsub,
                           layer=l, hd=hd, lam_init=lam_init)
        lru, conv_s, h_s = _lru_step(xr, xg, jnp.moveaxis(state_conv[l], 1, 0), state_h[l], *lru_w,
                                     pos0=past)
        x2 = _merge(att.reshape(dec_b, at_w), lru, wpa, wpb, gates, wo, x1)
        ys = _ffn(x2, g_ffn2[l][None], wup2, wdn2)
        outs[4].append(k32.reshape(dec_b, dec_t, n_heads, kw))
        outs[5].append(v32.reshape(dec_b, dec_t, n_heads, vd))
        outs[6].append(jnp.moveaxis(conv_s, 0, 1))
        outs[7].append(h_s)

    return (yp.reshape(batch, seq, d), ys.reshape(dec_b, dec_t, d),
            *[jnp.stack(o) for o in outs])
```

```python
import functools
import math

import jax
import jax.numpy as jnp
from jax import lax
from jax.experimental import pallas as pl
from jax.experimental.pallas import tpu as pltpu

EPS = 1e-6
LRU_C = 8.0
ROPE_THETA = 10000.0
LANES = 128
SUBLANES = 8
VMEM_LIMIT_BYTES = 56 * 1024 * 1024
FFN_BLOCK_ROWS = 512
FFN_BLOCK_HIDDEN = 512
DECODE_PAGES_PER_FFN_STEP = 4
LRU_TILE_ROWS = 256
DECODE_PAGES_PER_LRU_STEP = 8
LRU_VMEM_LIMIT_BYTES = 60 * 1024 * 1024

f32 = jnp.float32
bf16 = jnp.bfloat16


def _cparams(*sem):
    return pltpu.CompilerParams(dimension_semantics=sem, vmem_limit_bytes=VMEM_LIMIT_BYTES)


def _rms(x, g):
    return x * lax.rsqrt(jnp.mean(x * x, axis=-1, keepdims=True) + EPS) * g


def _lane_repeat(x, n):
    return x if n == 1 else jnp.concatenate([x] * n, axis=1)


def _gelu_tanh(x):
    c = math.sqrt(2.0 / math.pi)
    return 0.5 * x * (1.0 + jnp.tanh(c * (x + 0.044715 * (x * x * x))))


def _pick(n, pref):
    if n <= pref:
        return n
    b = pref
    while n % b:
        b //= 2
    return b


def _split_refs(refs, n_in, n_out, dec):
    if dec is None:
        return refs[:n_in], refs[n_in:n_in + n_out], refs[n_in + n_out:], None
    refs = refs[1:]
    nd = 5 + 2 * dec["pp"]
    o0 = n_in + nd
    return (refs[:n_in], refs[o0:o0 + n_out], refs[o0 + n_out + 1:-3],
            (refs[n_in:o0], refs[o0 + n_out], refs[-3:]))


def _ride_along(dec, side, step):
    ins, att_ref, (m_ref, l_ref, acc_ref) = side
    pp = dec["pp"]

    @pl.when(step == 0)
    def _():
        m_ref[...] = jnp.full(m_ref.shape, -jnp.inf, f32)
        l_ref[...] = jnp.zeros(l_ref.shape, f32)
        acc_ref[...] = jnp.zeros(acc_ref.shape, f32)

    def run(first, active):
        _decode_step(*ins[:5], ins[5:5 + pp], ins[5 + pp:], att_ref, m_ref, l_ref, acc_ref,
                     first=first, active=active, lam_init=dec["lam_init"])

    active = step // dec["spb"] < dec["nbatch"]
    if dec["inline"]:
        run(jnp.logical_and(active, step % dec["spb"] == 0), active)
    else:
        pl.when(active)(lambda: run(step % dec["spb"] == 0, True))


def _ride_along_operands(decode, nsteps, step_of):
    q, ck, cv, pt = decode["q"], decode["cache_k"], decode["cache_v"], decode["page_table"]
    _, nh, w = q.shape
    hd = w // 2
    layer, batch0, nbatch, pp = decode["layer"], decode["batch0"], decode["nbatch"], decode["pp"]
    page_rows = ck.shape[2]
    spb = pt.shape[1] // pp
    assert pt.shape[1] % pp == 0 and 0 < nbatch * spb <= nsteps

    def seq_of(*idx):
        return jnp.minimum(step_of(*idx) // spb, nbatch - 1)

    def page_spec(pi):
        def imap(*idx_pt):
            idx, pt_ref = idx_pt[:-1], idx_pt[-1]
            s = step_of(*idx)
            grp = jnp.where(s // spb < nbatch, s % spb, spb - 1)
            return (layer, pt_ref[batch0 + seq_of(*idx), grp * pp + pi], 0, 0)
        return pl.BlockSpec((1, 1, page_rows, w), imap)

    row_in = pl.BlockSpec((1, nh, w), lambda *a: (batch0 + seq_of(*a[:-1]), 0, 0))
    in_specs = [
        pl.BlockSpec((4, hd), lambda *a: (0, 0)),
        pl.BlockSpec((1, w), lambda *a: (0, 0)),
        row_in, row_in, row_in,
    ] + [page_spec(pi) for pi in range(pp)] * 2
    args = [decode["lam4"], decode["g_sub"], q, decode["k_new"], decode["v_new"]]
    args += [ck] * pp + [cv] * pp
    out_spec = pl.BlockSpec((1, nh, w), lambda *a: (seq_of(*a[:-1]), 0, 0))
    out_shape = jax.ShapeDtypeStruct((nbatch, nh, w), bf16)
    scratch = [pltpu.VMEM((2 * nh, LANES), f32), pltpu.VMEM((2 * nh, LANES), f32),
               pltpu.VMEM((2 * nh, w), f32)]
    inline = 10 * nbatch * spb >= 9 * nsteps
    static = dict(pp=pp, spb=spb, nbatch=nbatch, lam_init=decode["lam_init"], inline=inline)
    return in_specs, args, out_spec, out_shape, scratch, static


def _ffn_kernel(*refs, nj, emit_norm, dec):
    (x_ref, g_ref, wg_ref, wu_ref, wd_ref, gn_ref), outs, (h_ref,), side = _split_refs(
        refs, 6, 2 if emit_norm else 1, dec)
    o_ref = outs[0]
    j = pl.program_id(1)

    @pl.when(j == 0)
    def _():
        x = x_ref[...]
        h_ref[...] = _rms(x, g_ref[...]).astype(bf16)
        o_ref[...] = x

    if dec is not None:
        _ride_along(dec, side, pl.program_id(0) * nj + j)

    h = h_ref[...]
    gg = jnp.dot(h, wg_ref[...], preferred_element_type=f32)
    uu = jnp.dot(h, wu_ref[...], preferred_element_type=f32)
    act = (gg * jax.nn.sigmoid(gg)) * (uu * 0.5)
    o_ref[...] += jnp.dot(act.astype(bf16), wd_ref[...], preferred_element_type=f32)

    if emit_norm:
        @pl.when(j == nj - 1)
        def _():
            outs[1][...] = _rms(o_ref[...], gn_ref[...]).astype(bf16)


def _ffn_grid(m, ff):
    return m // _pick(m, FFN_BLOCK_ROWS), ff // _pick(ff, FFN_BLOCK_HIDDEN)


def _ffn(x, g, w_up, w_down, g_next=None, *, decode=None):
    m, d = x.shape
    ff = w_down.shape[0]
    ni, nj = _ffn_grid(m, ff)
    bm, bf = m // ni, ff // nj
    emit_norm = g_next is not None
    gn = g_next if emit_norm else g
    out_shape = [jax.ShapeDtypeStruct((m, d), f32)]
    out_specs = [pl.BlockSpec((bm, d), lambda i, j, *_: (i, 0))]
    if emit_norm:
        out_shape.append(jax.ShapeDtypeStruct((m, d), bf16))
        out_specs.append(pl.BlockSpec((bm, d), lambda i, j, *_: (i, 0)))
    in_specs = [
        pl.BlockSpec((bm, d), lambda i, j, *_: (i, 0)),
        pl.BlockSpec((1, d), lambda i, j, *_: (0, 0)),
        pl.BlockSpec((d, bf), lambda i, j, *_: (0, j)),
        pl.BlockSpec((d, bf), lambda i, j, *_: (0, j + nj)),
        pl.BlockSpec((bf, d), lambda i, j, *_: (j, 0)),
        pl.BlockSpec((1, d), lambda i, j, *_: (0, 0)),
    ]
    args = [x, g, w_up, w_up, w_down, gn]
    scratch = [pltpu.VMEM((bm, d), bf16)]
    dec = None
    prefetch = []
    if decode is not None:
        d_in, d_args, d_out, d_shape, d_scratch, dec = _ride_along_operands(
            decode, ni * nj, lambda i, j: i * nj + j)
        in_specs += d_in
        args += d_args
        out_specs.append(d_out)
        out_shape.append(d_shape)
        scratch += d_scratch
        prefetch = [decode["page_table"]]
    grid_spec = pltpu.PrefetchScalarGridSpec(
        num_scalar_prefetch=len(prefetch), grid=(ni, nj), in_specs=in_specs, out_specs=out_specs,
        scratch_shapes=scratch)
    sem = ("parallel" if dec is None else "arbitrary", "arbitrary")
    res = pl.pallas_call(
        functools.partial(_ffn_kernel, nj=nj, emit_norm=emit_norm, dec=dec),
        grid_spec=grid_spec,
        out_shape=out_shape,
        compiler_params=_cparams(*sem),
        name="ffn" if dec is None else "ffn_decode",
    )(*prefetch, *args)
    return res if len(res) > 1 else res[0]


def _proj_qk_kernel(a_ref, w_ref, cos_ref, sin_ref, *o_refs, sub, out_dtypes):
    a = a_ref[...]
    cos, sin = cos_ref[...], sin_ref[...]
    for s0 in range(0, w_ref.shape[1], sub):
        z = jnp.dot(a, w_ref[:, s0:s0 + sub], preferred_element_type=f32)
        for n in range(sub // LANES):
            x = z[:, n * LANES:(n + 1) * LANES]
            inv = lax.rsqrt(jnp.mean(x * x, axis=-1, keepdims=True) + EPS)
            r = (x * cos + pltpu.roll(x, LANES // 2, axis=1) * sin) * inv
            c0 = s0 + n * LANES
            for o_ref, dt in zip(o_refs, out_dtypes):
                o_ref[:, c0:c0 + LANES] = r.astype(dt)


def _proj_plain_kernel(a_ref, w_ref, *o_refs, act, out_dtypes):
    z = jnp.dot(a_ref[...], w_ref[...], preferred_element_type=f32)
    if act == "sigmoid":
        z = jax.nn.sigmoid(z)
    for o_ref, dt in zip(o_refs, out_dtypes):
        o_ref[...] = z.astype(dt)


def _proj(hn, w_in, col0, ncols, out_dtypes, *, qk=None, act=None, bm=1024, bn=1024):
    m, d = hn.shape
    if qk is None and len(out_dtypes) == 1:
        bm = 2 * bm
    bm = _pick(m, bm)
    if qk is not None:
        bm = _pick(qk[0].shape[0], bm)
    bn = _pick(ncols, bn)
    while col0 % bn:
        bn //= 2
    joff = col0 // bn
    in_specs = [
        pl.BlockSpec((bm, d), lambda i, j: (i, 0)),
        pl.BlockSpec((d, bn), lambda i, j: (0, j + joff)),
    ]
    args = [hn, w_in]
    if qk is not None:
        cos, sin = qk
        npos = cos.shape[0] // bm
        in_specs += [
            pl.BlockSpec((bm, LANES), lambda i, j: (i % npos, 0)),
            pl.BlockSpec((bm, LANES), lambda i, j: (i % npos, 0)),
        ]
        args += [cos, sin]
        body = functools.partial(_proj_qk_kernel, sub=_pick(bn, 2 * LANES), out_dtypes=out_dtypes)
    else:
        body = functools.partial(_proj_plain_kernel, act=act, out_dtypes=out_dtypes)
    return pl.pallas_call(
        body,
        grid=(m // bm, ncols // bn),
        in_specs=in_specs,
        out_specs=[pl.BlockSpec((bm, bn), lambda i, j: (i, j)) for _ in out_dtypes],
        out_shape=[jax.ShapeDtypeStruct((m, ncols), dt) for dt in out_dtypes],
        compiler_params=_cparams("parallel", "arbitrary"),
        name="proj",
    )(*args)


def _lambda(lam_ref, lam_init):
    r = lam_ref[...]
    s1 = jnp.sum(r[0:1] * r[1:2], axis=-1, keepdims=True)
    s2 = jnp.sum(r[2:3] * r[3:4], axis=-1, keepdims=True)
    return jnp.exp(s1) - jnp.exp(s2) + lam_init


def _attn_prompt_kernel(lam_ref, gsub_ref, q_ref, k_ref, v_ref, o_ref,
                        m_ref, l_ref, acc_ref, *, tk, hd, lam_init):
    qi = pl.program_id(2)
    vd = v_ref.shape[1]
    nparts = q_ref.shape[0] // tk
    nrep_s = tk // LANES
    nrep_v = vd // LANES

    m_ref[...] = jnp.full(m_ref.shape, -jnp.inf, f32)
    l_ref[...] = jnp.zeros(l_ref.shape, f32)
    acc_ref[...] = jnp.zeros(acc_ref.shape, f32)

    def kv_block(j, parts):
        r0 = pl.multiple_of(j * tk, tk)
        v = v_ref[pl.ds(r0, tk), :]
        for c in range(2):
            k = k_ref[pl.ds(r0, tk), c * hd:(c + 1) * hd]
            for part, diagonal in parts:
                rows = slice(part * tk, (part + 1) * tk)
                q = q_ref[rows, c * hd:(c + 1) * hd]
                s = lax.dot_general(q, k, (((1,), (1,)), ((), ())), preferred_element_type=f32)
                if diagonal:
                    row = lax.broadcasted_iota(jnp.int32, s.shape, 0)
                    col = lax.broadcasted_iota(jnp.int32, s.shape, 1)
                    s = jnp.where(row >= col, s, -jnp.inf)
                m_prev = m_ref[c, rows]
                m_new = jnp.maximum(m_prev, jnp.max(s, axis=1, keepdims=True))
                alpha = jnp.exp2(m_prev - m_new)
                p = jnp.exp2(s - _lane_repeat(m_new, nrep_s))
                l_ref[c, rows] = alpha * l_ref[c, rows] + jnp.sum(p, axis=1, keepdims=True)
                acc_ref[c, rows] = (acc_ref[c, rows] * _lane_repeat(alpha, nrep_v)
                                    + jnp.dot(p.astype(bf16), v, preferred_element_type=f32))
                m_ref[c, rows] = m_new

    def body(j, carry):
        kv_block(j, [(part, False) for part in range(nparts)])
        return carry

    lax.fori_loop(0, nparts * qi, body, 0)
    for d in range(nparts):
        kv_block(nparts * qi + d, [(part, part == d) for part in range(d, nparts)])
    o1 = acc_ref[0] / _lane_repeat(l_ref[0], nrep_v)
    o2 = acc_ref[1] / _lane_repeat(l_ref[1], nrep_v)
    d = o1 - _lambda(lam_ref, lam_init) * o2
    o_ref[...] = (_rms(d, gsub_ref[...]) * (1.0 - lam_init)).astype(o_ref.dtype)


def _attn_prompt(q, k, v, lam4, g_sub, *, batch, seq, n_heads, hd, vd, lam_init, tq=1024, tk=512):
    tq = _pick(seq, tq)
    tk = _pick(tq, tk)
    nq = seq // tq
    return pl.pallas_call(
        functools.partial(_attn_prompt_kernel, tk=tk, hd=hd, lam_init=lam_init),
        grid=(batch, n_heads, nq),
        in_specs=[
            pl.BlockSpec((4, hd), lambda b, h, i: (0, 0)),
            pl.BlockSpec((1, vd), lambda b, h, i: (0, 0)),
            pl.BlockSpec((tq, 2 * hd), lambda b, h, i: (b * nq + i, h)),
            pl.BlockSpec((seq, 2 * hd), lambda b, h, i: (b, h)),
            pl.BlockSpec((seq, vd), lambda b, h, i: (b, h)),
        ],
        out_specs=pl.BlockSpec((tq, vd), lambda b, h, i: (b * nq + i, h)),
        out_shape=jax.ShapeDtypeStruct((batch * seq, n_heads * vd), bf16),
        scratch_shapes=[
            pltpu.VMEM((2, tq, LANES), f32),
            pltpu.VMEM((2, tq, LANES), f32),
            pltpu.VMEM((2, tq, vd), f32),
        ],
        compiler_params=_cparams("parallel", "parallel", "arbitrary"),
        name="attn_prompt",
    )(lam4, g_sub, q, k, v)


def _decode_step(lam_ref, gsub_ref, q_ref, kn_ref, vn_ref, k_refs, v_refs, o_ref,
                 m_ref, l_ref, acc_ref, *, first, active, lam_init):
    nh, w = q_ref.shape[1], q_ref.shape[2]
    hd = w // 2
    rows = k_refs[0].shape[2]
    nrep_w = w // LANES

    q = q_ref[0]
    lane = lax.broadcasted_iota(jnp.int32, (nh, w), 1)
    qm = jnp.concatenate([jnp.where(lane < hd, q, 0.0), jnp.where(lane >= hd, q, 0.0)], axis=0)
    r_i = lax.broadcasted_iota(jnp.int32, (2 * nh, rows), 0)
    c_i = lax.broadcasted_iota(jnp.int32, (2 * nh, rows), 1)
    valid = (c_i % nh) == jnp.where(active, r_i % nh, -1)
    scores = []
    for k_ref in k_refs:
        s = lax.dot_general(qm, k_ref[0, 0], (((1,), (1,)), ((), ())), preferred_element_type=f32)
        scores.append(jnp.where(valid, s, -jnp.inf))
    mc = scores[0]
    for s in scores[1:]:
        mc = jnp.maximum(mc, s)
    m_prev = jnp.where(first, -jnp.inf, m_ref[...])
    l_prev = jnp.where(first, 0.0, l_ref[...])
    acc_prev = jnp.where(first, 0.0, acc_ref[...])
    m_new = jnp.maximum(m_prev, jnp.max(mc, axis=1, keepdims=True))
    alpha = jnp.exp2(m_prev - m_new)
    m_b = _lane_repeat(m_new, rows // LANES)
    lsum = None
    pv = None
    for s, v_ref in zip(scores, v_refs):
        p = jnp.exp2(s - m_b)
        ls = jnp.sum(p, axis=1, keepdims=True)
        d = jnp.dot(p, v_ref[0, 0], preferred_element_type=f32)
        lsum = ls if lsum is None else lsum + ls
        pv = d if pv is None else pv + d
    l_new = alpha * l_prev + lsum
    acc_new = _lane_repeat(alpha, nrep_w) * acc_prev + pv
    m_ref[...] = m_new
    l_ref[...] = l_new
    acc_ref[...] = acc_new

    kn = jnp.concatenate([kn_ref[0], kn_ref[0]], axis=0)
    vn = jnp.concatenate([vn_ref[0], vn_ref[0]], axis=0)
    s_self = jnp.sum(qm * kn, axis=1, keepdims=True)
    m_f = jnp.maximum(m_new, s_self)
    a_f = jnp.exp2(m_new - m_f)
    p_f = jnp.exp2(s_self - m_f)
    l_f = a_f * l_new + p_f
    acc_f = _lane_repeat(a_f, nrep_w) * acc_new + _lane_repeat(p_f, nrep_w) * vn
    o = acc_f / _lane_repeat(l_f, nrep_w)
    d = o[0:nh] - _lambda(lam_ref, lam_init) * o[nh:2 * nh]
    o_ref[0] = (_rms(d, gsub_ref[...]) * (1.0 - lam_init)).astype(o_ref.dtype)


def _attn_decode_kernel(pt_ref, lam_ref, gsub_ref, q_ref, kn_ref, vn_ref, *rest, pp, lam_init):
    k_refs, v_refs = rest[:pp], rest[pp:2 * pp]
    o_ref, m_ref, l_ref, acc_ref = rest[2 * pp:]
    g = pl.program_id(1)

    @pl.when(jnp.logical_and(pl.program_id(0) == 0, g == 0))
    def _():
        m_ref[...] = jnp.full(m_ref.shape, -jnp.inf, f32)
        l_ref[...] = jnp.zeros(l_ref.shape, f32)
        acc_ref[...] = jnp.zeros(acc_ref.shape, f32)

    _decode_step(lam_ref, gsub_ref, q_ref, kn_ref, vn_ref, k_refs, v_refs, o_ref,
                 m_ref, l_ref, acc_ref, first=g == 0, active=True, lam_init=lam_init)


def _attn_decode(decode, pp=8):
    q, cache_k, cache_v, page_table = (decode[k] for k in ("q", "cache_k", "cache_v", "page_table"))
    layer, batch0, nbatch = decode["layer"], decode["batch0"], decode["nbatch"]
    _, nh, w = q.shape
    hd = w // 2
    n_pages = page_table.shape[1]
    pp = _pick(n_pages, pp)
    page_rows = cache_k.shape[2]

    def page_spec(pi):
        return pl.BlockSpec((1, 1, page_rows, w),
                            lambda i, g, pt, pi=pi: (layer, pt[batch0 + i, g * pp + pi], 0, 0))

    row_in = pl.BlockSpec((1, nh, w), lambda i, g, pt: (batch0 + i, 0, 0))
    grid_spec = pltpu.PrefetchScalarGridSpec(
        num_scalar_prefetch=1,
        grid=(nbatch, n_pages // pp),
        in_specs=[
            pl.BlockSpec((4, hd), lambda i, g, pt: (0, 0)),
            pl.BlockSpec((1, w), lambda i, g, pt: (0, 0)),
            row_in, row_in, row_in,
        ] + [page_spec(pi) for pi in range(pp)] * 2,
        out_specs=pl.BlockSpec((1, nh, w), lambda i, g, pt: (i, 0, 0)),
        scratch_shapes=[
            pltpu.VMEM((2 * nh, LANES), f32),
            pltpu.VMEM((2 * nh, LANES), f32),
            pltpu.VMEM((2 * nh, w), f32),
        ],
    )
    return pl.pallas_call(
        functools.partial(_attn_decode_kernel, pp=pp, lam_init=decode["lam_init"]),
        grid_spec=grid_spec,
        out_shape=jax.ShapeDtypeStruct((nbatch, nh, w), bf16),
        compiler_params=_cparams("arbitrary", "arbitrary"),
        name="attn_decode",
    )(page_table, decode["lam4"], decode["g_sub"], q, decode["k_new"], decode["v_new"],
      *([cache_k] * pp), *([cache_v] * pp))


def _lru_gates(xc, wa_ref, ba_ref, wx_ref, bx_ref, ap_ref):
    nb, blk = wa_ref.shape[0], wa_ref.shape[1]
    xcb = xc.astype(bf16)
    za = jnp.concatenate(
        [jnp.dot(xcb[:, n * blk:(n + 1) * blk], wa_ref[n], preferred_element_type=f32)
         for n in range(nb)], axis=1)
    zx = jnp.concatenate(
        [jnp.dot(xcb[:, n * blk:(n + 1) * blk], wx_ref[n], preferred_element_type=f32)
         for n in range(nb)], axis=1)
    gate_a = jax.nn.sigmoid(za + ba_ref[...])
    gate_x = jax.nn.sigmoid(zx + bx_ref[...])
    ap = -ap_ref[...]
    softplus = jnp.maximum(ap, 0.0) + jnp.log1p(jnp.exp(-jnp.abs(ap)))
    log_a = (-LRU_C) * gate_a * softplus
    a = jnp.exp(log_a)
    mult = jnp.sqrt(-jnp.tanh(log_a) * (a * a + 1.0))
    return a, gate_x, mult


def _lru_prompt_kernel(*refs, tt, pos0, dec):
    ((xr_ref, xg_ref, cw_ref, cb_ref, wa_ref, ba_ref, wx_ref, bx_ref, ap_ref),
     (o_ref, conv_ref, h_ref), (xprev_ref, hc_ref, a_s, b_s), side) = _split_refs(refs, 9, 3, dec)
    t = pl.program_id(1)
    nt = pl.num_programs(1)
    w = xr_ref.shape[1]
    cwid = cw_ref.shape[0]
    assert cwid <= SUBLANES + 1
    ng = tt // SUBLANES

    @pl.when(t == 0)
    def _():
        xprev_ref[...] = jnp.zeros(xprev_ref.shape, f32)
        hc_ref[...] = jnp.zeros(hc_ref.shape, f32)

    if dec is not None:
        _ride_along(dec, side, pl.program_id(0) * nt + t)

    x = xr_ref[...].reshape(ng, SUBLANES, w)
    prev = xprev_ref[...]
    sub = lax.broadcasted_iota(jnp.int32, (ng, SUBLANES, w), 1)
    xc = x * cw_ref[cwid - 1:cwid, :]
    for i in range(1, cwid):
        r = pltpu.roll(x, i, axis=1)
        r_prev = jnp.concatenate([pltpu.roll(prev, i, axis=0)[None], r[:ng - 1]], axis=0)
        xc = xc + jnp.where(sub >= i, r, r_prev) * cw_ref[cwid - 1 - i:cwid - i, :]
    xc = (xc + cb_ref[...]).reshape(tt, w)
    xprev_ref[...] = x[ng - 1]

    a, gate_x, mult = _lru_gates(xc, wa_ref, ba_ref, wx_ref, bx_ref, ap_ref)
    if pos0 == 0:
        row = lax.broadcasted_iota(jnp.int32, (tt, w), 0)
        mult = jnp.where((row + t * tt) == 0, 1.0, mult)
    b = xc * gate_x * mult

    a = a.reshape(ng, SUBLANES, w)
    b = b.reshape(ng, SUBLANES, w)
    for s in (1, 2, 4):
        keep = sub >= s
        b = jnp.where(keep, a * pltpu.roll(b, s, axis=1) + b, b)
        a = jnp.where(keep, a * pltpu.roll(a, s, axis=1), a)
    a_s[...] = a.reshape(tt, w)
    b_s[...] = b.reshape(tt, w)

    def grp(gi, h):
        r0 = pl.multiple_of(gi * SUBLANES, SUBLANES)
        hg = a_s[pl.ds(r0, SUBLANES), :] * h + b_s[pl.ds(r0, SUBLANES), :]
        b_s[pl.ds(r0, SUBLANES), :] = hg
        return jnp.broadcast_to(hg[SUBLANES - 1:SUBLANES, :], (SUBLANES, w))

    h = lax.fori_loop(0, tt // SUBLANES, grp, hc_ref[...])
    hc_ref[...] = h
    o_ref[...] = (b_s[...] * _gelu_tanh(xg_ref[...].astype(f32))).astype(o_ref.dtype)

    @pl.when(t == nt - 1)
    def _():
        conv_ref[0] = xprev_ref[SUBLANES - (cwid - 1):SUBLANES, :]
        h_ref[0] = h[0:1, :]


def _lru_grid(batch, seq):
    return batch, seq // _pick(seq, LRU_TILE_ROWS)


def _lru_prompt(xr, xg, conv_w, conv_b, w_a, b_a, w_x, b_x, a_param, *, batch, seq, pos0,
                decode=None):
    w = xr.shape[1]
    _, nt = _lru_grid(batch, seq)
    tt = seq // nt
    cwid = conv_w.shape[0]
    vec = pl.BlockSpec((1, w), lambda b, t, *_: (0, 0))
    wspec = pl.BlockSpec(w_a.shape, lambda b, t, *_: (0, 0, 0))
    in_specs = [
        pl.BlockSpec((tt, w), lambda b, t, *_: (b * nt + t, 0)),
        pl.BlockSpec((tt, w), lambda b, t, *_: (b * nt + t, 0)),
        pl.BlockSpec((cwid, w), lambda b, t, *_: (0, 0)),
        vec, wspec, vec, wspec, vec, vec,
    ]
    args = [xr, xg, conv_w, conv_b, w_a, b_a, w_x, b_x, a_param]
    out_specs = [
        pl.BlockSpec((tt, w), lambda b, t, *_: (b * nt + t, 0)),
        pl.BlockSpec((1, cwid - 1, w), lambda b, t, *_: (b, 0, 0)),
        pl.BlockSpec((1, 1, w), lambda b, t, *_: (b, 0, 0)),
    ]
    out_shape = [
        jax.ShapeDtypeStruct((batch * seq, w), bf16),
        jax.ShapeDtypeStruct((batch, cwid - 1, w), f32),
        jax.ShapeDtypeStruct((batch, 1, w), f32),
    ]
    scratch = [
        pltpu.VMEM((SUBLANES, w), f32),
        pltpu.VMEM((SUBLANES, w), f32),
        pltpu.VMEM((tt, w), f32),
        pltpu.VMEM((tt, w), f32),
    ]
    dec = None
    prefetch = []
    if decode is not None:
        d_in, d_args, d_out, d_shape, d_scratch, dec = _ride_along_operands(
            decode, batch * nt, lambda b, t: b * nt + t)
        in_specs += d_in
        args += d_args
        out_specs.append(d_out)
        out_shape.append(d_shape)
        scratch += d_scratch
        prefetch = [decode["page_table"]]
    return pl.pallas_call(
        functools.partial(_lru_prompt_kernel, tt=tt, pos0=pos0, dec=dec),
        grid_spec=pltpu.PrefetchScalarGridSpec(
            num_scalar_prefetch=len(prefetch), grid=(batch, nt), in_specs=in_specs,
            out_specs=out_specs, scratch_shapes=scratch),
        out_shape=out_shape,
        compiler_params=pltpu.CompilerParams(
            dimension_semantics=("parallel" if dec is None else "arbitrary", "arbitrary"),
            vmem_limit_bytes=LRU_VMEM_LIMIT_BYTES),
        name="lru_prompt" if dec is None else "lru_decode",
    )(*prefetch, *args)


def _lru_step_kernel(xr_ref, xg_ref, buf_ref, h0_ref, cw_ref, cb_ref, wa_ref, ba_ref, wx_ref,
                     bx_ref, ap_ref, o_ref, conv_ref, h_ref, *, pos0):
    cwid = cw_ref.shape[0]
    x = xr_ref[...]
    xc = cb_ref[...] + x * cw_ref[cwid - 1:cwid, :]
    for i in range(cwid - 1):
        xc = xc + buf_ref[i] * cw_ref[i:i + 1, :]
        if i > 0:
            conv_ref[i - 1] = buf_ref[i]
    conv_ref[cwid - 2] = x
    a, gate_x, mult = _lru_gates(xc, wa_ref, ba_ref, wx_ref, bx_ref, ap_ref)
    if pos0 == 0:
        mult = jnp.ones_like(mult)
    h = a * h0_ref[...] + xc * gate_x * mult
    h_ref[...] = h
    o_ref[...] = (h * _gelu_tanh(xg_ref[...].astype(f32))).astype(o_ref.dtype)


def _lru_step(xr, xg, buf, h0, conv_w, conv_b, w_a, b_a, w_x, b_x, a_param, *, pos0):
    b, w = xr.shape
    cwid = conv_w.shape[0]
    vec = pl.BlockSpec((1, w), lambda i: (0, 0))
    wspec = pl.BlockSpec(w_a.shape, lambda i: (0, 0, 0))
    return pl.pallas_call(
        functools.partial(_lru_step_kernel, pos0=pos0),
        grid=(1,),
        in_specs=[
            pl.BlockSpec((b, w), lambda i: (0, 0)),
            pl.BlockSpec((b, w), lambda i: (0, 0)),
            pl.BlockSpec((cwid - 1, b, w), lambda i: (0, 0, 0)),
            pl.BlockSpec((b, w), lambda i: (0, 0)),
            pl.BlockSpec((cwid, w), lambda i: (0, 0)),
            vec, wspec, vec, wspec, vec, vec,
        ],
        out_specs=[
            pl.BlockSpec((b, w), lambda i: (0, 0)),
            pl.BlockSpec((cwid - 1, b, w), lambda i: (0, 0, 0)),
            pl.BlockSpec((b, w), lambda i: (0, 0)),
        ],
        out_shape=[
            jax.ShapeDtypeStruct((b, w), bf16),
            jax.ShapeDtypeStruct((cwid - 1, b, w), f32),
            jax.ShapeDtypeStruct((b, w), f32),
        ],
        compiler_params=_cparams("arbitrary"),
        name="lru_step",
    )(xr, xg, buf, h0, conv_w, conv_b, w_a, b_a, w_x, b_x, a_param)


def _merge_kernel(a_ref, b_ref, wpa_ref, wpb_ref, ga_ref, gb_ref, wo_ref, x_ref, o_ref):
    j = pl.program_id(1)

    @pl.when(j == 0)
    def _():
        o_ref[...] = x_ref[...]

    ya = jnp.dot(a_ref[...], wpa_ref[...], preferred_element_type=f32)
    yb = jnp.dot(b_ref[...], wpb_ref[...], preferred_element_type=f32)
    y = ga_ref[...].astype(f32) * ya + gb_ref[...].astype(f32) * yb
    o_ref[...] += jnp.dot(y.astype(bf16), wo_ref[...], preferred_element_type=f32)


def _merge(a, b, w_pa, w_pb, gates, w_o, x, *, bm=512, bn=512):
    m, d = x.shape
    ka = a.shape[1]
    kb = b.shape[1]
    bm = _pick(m, bm)
    bn = _pick(d, bn)
    nj = d // bn
    return pl.pallas_call(
        _merge_kernel,
        grid=(m // bm, nj),
        in_specs=[
            pl.BlockSpec((bm, ka), lambda i, j: (i, 0)),
            pl.BlockSpec((bm, kb), lambda i, j: (i, 0)),
            pl.BlockSpec((ka, bn), lambda i, j: (0, j)),
            pl.BlockSpec((kb, bn), lambda i, j: (0, j)),
            pl.BlockSpec((bm, bn), lambda i, j: (i, j)),
            pl.BlockSpec((bm, bn), lambda i, j: (i, j + nj)),
            pl.BlockSpec((bn, d), lambda i, j: (j, 0)),
            pl.BlockSpec((bm, d), lambda i, j: (i, 0)),
        ],
        out_specs=pl.BlockSpec((bm, d), lambda i, j: (i, 0)),
        out_shape=jax.ShapeDtypeStruct((m, d), f32),
        compiler_params=_cparams("parallel", "arbitrary"),
        name="merge",
    )(a, b, w_pa, w_pb, gates, gates, w_o, x)


def _rope_tables(positions, hd):
    half = hd // 2
    inv = 1.0 / (ROPE_THETA ** (jnp.arange(half, dtype=f32) / half))
    ang = positions.astype(f32)[:, None] * inv[None, :]
    cos, sin = jnp.cos(ang), jnp.sin(ang)
    return jnp.concatenate([cos, cos], axis=-1), jnp.concatenate([-sin, sin], axis=-1)


def _lambda_init(layer):
    return 0.8 - 0.6 * math.exp(-0.3 * layer)


def kernel(x_prompt, x_sample, cache_k, cache_v, state_conv, state_h, page_table, g_ffn1, w_up1, w_down1, g_mix, w_in, g_q, g_k, lam_q1, lam_k1, lam_q2, lam_k2, g_sub, conv_w, conv_b, w_a, b_a, w_x, b_x, a_param, w_pa, w_pb, w_o, g_ffn2, w_up2, w_down2):
    batch, seq, d = x_prompt.shape
    dec_b, dec_t, _ = x_sample.shape
    assert dec_t == 1, "sample group decodes one token per sequence"
    depth, n_pool, page, n_heads, kw = cache_k.shape
    hd = g_q.shape[1]
    vd = g_sub.shape[1]
    assert kw == 2 * hd and vd == 2 * hd
    lw = conv_w.shape[2]
    qk_w = n_heads * 2 * hd
    at_w = n_heads * vd
    past = page_table.shape[1] * page
    scale = hd ** -0.5 * math.log2(math.e)

    rope_p = _rope_tables(jnp.arange(seq), hd)
    rope_s = _rope_tables(jnp.full((dec_b,), past), hd)

    def fold(rope, g, s):
        return rope[0] * (g * s), rope[1] * (jnp.roll(g, hd // 2, axis=-1) * s)

    yp = x_prompt.reshape(batch * seq, d)
    ys = x_sample.reshape(dec_b, d)
    outs = [[] for _ in range(8)]
    for l in range(depth):
        lam_init = _lambda_init(l)
        wup1, wdn1 = w_up1[l].astype(bf16), w_down1[l].astype(bf16)
        wup2, wdn2 = w_up2[l].astype(bf16), w_down2[l].astype(bf16)
        win = w_in[l].astype(bf16)
        wpa, wpb, wo = w_pa[l].astype(bf16), w_pb[l].astype(bf16), w_o[l].astype(bf16)
        wa, wx = w_a[l].astype(bf16), w_x[l].astype(bf16)
        lam4 = jnp.concatenate([lam_q1[l][None], lam_k1[l][None], lam_q2[l][None], lam_k2[l][None]], 0)
        gsub, gq, gk = g_sub[l][None], g_q[l][None], g_k[l][None]
        lru_w = (conv_w[l], conv_b[l][None], wa, b_a[l][None], wx, b_x[l][None], a_param[l][None])
        ck = cache_k.reshape(depth, n_pool, page * n_heads, kw)
        cv = cache_v.reshape(depth, n_pool, page * n_heads, vd)

        def projections(hn, rope, q_dtype):
            (q,) = _proj(hn, win, 0, qk_w, [q_dtype], qk=fold(rope, gq, scale))
            k32, k16 = _proj(hn, win, qk_w, qk_w, [f32, bf16], qk=fold(rope, gk, 1.0))
            v32, v16 = _proj(hn, win, 2 * qk_w, at_w, [f32, bf16])
            (xr,) = _proj(hn, win, 2 * qk_w + at_w, lw, [f32])
            (xg,) = _proj(hn, win, 2 * qk_w + at_w + lw, lw, [bf16])
            (gates,) = _proj(hn, win, 2 * qk_w + at_w + 2 * lw, 2 * d, [bf16], act="sigmoid")
            return q, k32, k16, v32, v16, xr, xg, gates

        x1_s, hn_s = _ffn(ys, g_ffn1[l][None], wup1, wdn1, g_mix[l][None])
        q_s, k32_s, _, v32_s, _, xr_s, xg_s, gates_s = projections(hn_s, rope_s, f32)
        n_pages = page_table.shape[1]
        dec = dict(q=q_s.reshape(dec_b, n_heads, kw), k_new=k32_s.reshape(dec_b, n_heads, kw),
                   v_new=v32_s.reshape(dec_b, n_heads, vd), cache_k=ck, cache_v=cv,
                   page_table=page_table, lam4=lam4, g_sub=gsub, layer=l, lam_init=lam_init)
        pp_lru = _pick(n_pages, DECODE_PAGES_PER_LRU_STEP)
        pp_ffn = _pick(n_pages, DECODE_PAGES_PER_FFN_STEP)
        lb, lt = _lru_grid(batch, seq)
        ni, nj = _ffn_grid(batch * seq, wdn1.shape[0])
        n_lru = min(dec_b, (lb * lt) // (n_pages // pp_lru))
        fit = (ni * nj) // (n_pages // pp_ffn)
        n1 = min(fit, dec_b - n_lru)
        n2 = min(fit, dec_b - n_lru - n1)
        n3 = dec_b - n_lru - n1 - n2
        att_s = []

        def host(fn, *args, nbatch, pp, **kw):
            if not nbatch:
                return fn(*args, **kw)
            batch0 = sum(a.shape[0] for a in att_s)
            *res, att_part = fn(*args, decode=dict(dec, batch0=batch0, nbatch=nbatch, pp=pp), **kw)
            att_s.append(att_part)
            return res[0] if len(res) == 1 else res

        x1, hn = host(_ffn, yp, g_ffn1[l][None], wup1, wdn1, g_mix[l][None], nbatch=n1, pp=pp_ffn)
        q, k32, k16, v32, v16, xr, xg, gates = projections(hn, rope_p, bf16)
        att = _attn_prompt(q, k16, v16, lam4, gsub, batch=batch, seq=seq, n_heads=n_heads,
                           hd=hd, vd=vd, lam_init=lam_init)
        lru, conv_p, h_p = host(_lru_prompt, xr, xg, *lru_w, nbatch=n_lru, pp=pp_lru,
                                batch=batch, seq=seq, pos0=0)
        x2 = _merge(att, lru, wpa, wpb, gates, wo, x1)
        yp = host(_ffn, x2, g_ffn2[l][None], wup2, wdn2, nbatch=n2, pp=pp_ffn)
        if n3:
            batch0 = sum(a.shape[0] for a in att_s)
            att_s.append(_attn_decode(dict(dec, batch0=batch0, nbatch=n3)))
        outs[0].append(k32.reshape(batch, seq, n_heads, kw))
        outs[1].append(v32.reshape(batch, seq, n_heads, vd))
        outs[2].append(conv_p)
        outs[3].append(h_p.reshape(batch, lw))

        att = jnp.concatenate(att_s, axis=0) if len(att_s) > 1 else att_s[0]
        lru, conv_s, h_s = _lru_step(xr_s, xg_s, jnp.moveaxis(state_conv[l], 1, 0), state_h[l],
                                     *lru_w, pos0=past)
        x2 = _merge(att.reshape(dec_b, at_w), lru, wpa, wpb, gates_s, wo, x1_s)
        ys = _ffn(x2, g_ffn2[l][None], wup2, wdn2)
        outs[4].append(k32_s.reshape(dec_b, dec_t, n_heads, kw))
        outs[5].append(v32_s.reshape(dec_b, dec_t, n_heads, vd))
        outs[6].append(jnp.moveaxis(conv_s, 0, 1))
        outs[7].append(h_s)

    return (yp.reshape(batch, seq, d), ys.reshape(dec_b, dec_t, d),
            *[jnp.stack(o) for o in outs])
```

```python
import functools
import math

import jax
import jax.numpy as jnp
from jax import lax
from jax.experimental import pallas as pl
from jax.experimental.pallas import tpu as pltpu

EPS = 1e-6
LRU_C = 8.0
ROPE_THETA = 10000.0
LANES = 128
SUBLANES = 8
VMEM_LIMIT_BYTES = 56 * 1024 * 1024
FFN_BLOCK_ROWS = 512
FFN_BLOCK_HIDDEN = 512
DECODE_PAGES_PER_FFN_STEP = 4
LRU_TILE_ROWS = 128
DECODE_PAGES_PER_LRU_STEP = 8
ATTN_BLOCK_Q = 1024
ATTN_BLOCK_K = 512
DECODE_PAGES_PER_ATTN_STEP = 8
LRU_VMEM_LIMIT_BYTES = 60 * 1024 * 1024

f32 = jnp.float32
bf16 = jnp.bfloat16


def _cparams(*sem):
    return pltpu.CompilerParams(dimension_semantics=sem, vmem_limit_bytes=VMEM_LIMIT_BYTES)


def _rms(x, g):
    return x * lax.rsqrt(jnp.mean(x * x, axis=-1, keepdims=True) + EPS) * g


def _lane_repeat(x, n):
    return x if n == 1 else jnp.concatenate([x] * n, axis=1)


def _gelu_tanh(x):
    c = math.sqrt(2.0 / math.pi)
    return 0.5 * x * (1.0 + jnp.tanh(c * (x + 0.044715 * (x * x * x))))


def _pick(n, pref):
    if n <= pref:
        return n
    b = pref
    while n % b:
        b //= 2
    return b


def _split_refs(refs, n_in, n_out, dec):
    if dec is None:
        return refs[:n_in], refs[n_in:n_in + n_out], refs[n_in + n_out:], None
    refs = refs[1:]
    nd = 5 + 2 * dec["pp"]
    o0 = n_in + nd
    return (refs[:n_in], refs[o0:o0 + n_out], refs[o0 + n_out + 1:-3],
            (refs[n_in:o0], refs[o0 + n_out], refs[-3:]))


def _ride_along(dec, side, step):
    ins, att_ref, (m_ref, l_ref, acc_ref) = side
    pp = dec["pp"]

    @pl.when(step == 0)
    def _():
        m_ref[...] = jnp.full(m_ref.shape, -jnp.inf, f32)
        l_ref[...] = jnp.zeros(l_ref.shape, f32)
        acc_ref[...] = jnp.zeros(acc_ref.shape, f32)

    def run(first, active):
        _decode_step(*ins[:5], ins[5:5 + pp], ins[5 + pp:], att_ref, m_ref, l_ref, acc_ref,
                     first=first, active=active, lam_init=dec["lam_init"])

    active = step // dec["spb"] < dec["nbatch"]
    if dec["inline"]:
        run(jnp.logical_and(active, step % dec["spb"] == 0), active)
    else:
        pl.when(active)(lambda: run(step % dec["spb"] == 0, True))


def _ride_along_operands(decode, nsteps, step_of):
    q, ck, cv, pt = decode["q"], decode["cache_k"], decode["cache_v"], decode["page_table"]
    _, nh, w = q.shape
    hd = w // 2
    layer, batch0, nbatch, pp = decode["layer"], decode["batch0"], decode["nbatch"], decode["pp"]
    page_rows = ck.shape[2]
    spb = pt.shape[1] // pp
    assert pt.shape[1] % pp == 0 and 0 < nbatch * spb <= nsteps

    def seq_of(*idx):
        return jnp.minimum(step_of(*idx) // spb, nbatch - 1)

    def page_spec(pi):
        def imap(*idx_pt):
            idx, pt_ref = idx_pt[:-1], idx_pt[-1]
            s = step_of(*idx)
            grp = jnp.where(s // spb < nbatch, s % spb, spb - 1)
            return (layer, pt_ref[batch0 + seq_of(*idx), grp * pp + pi], 0, 0)
        return pl.BlockSpec((1, 1, page_rows, w), imap)

    row_in = pl.BlockSpec((1, nh, w), lambda *a: (batch0 + seq_of(*a[:-1]), 0, 0))
    in_specs = [
        pl.BlockSpec((4, hd), lambda *a: (0, 0)),
        pl.BlockSpec((1, w), lambda *a: (0, 0)),
        row_in, row_in, row_in,
    ] + [page_spec(pi) for pi in range(pp)] * 2
    args = [decode["lam4"], decode["g_sub"], q, decode["k_new"], decode["v_new"]]
    args += [ck] * pp + [cv] * pp
    out_spec = pl.BlockSpec((1, nh, w), lambda *a: (seq_of(*a[:-1]), 0, 0))
    out_shape = jax.ShapeDtypeStruct((nbatch, nh, w), bf16)
    scratch = [pltpu.VMEM((2 * nh, LANES), f32), pltpu.VMEM((2 * nh, LANES), f32),
               pltpu.VMEM((2 * nh, w), f32)]
    inline = 10 * nbatch * spb >= 9 * nsteps
    static = dict(pp=pp, spb=spb, nbatch=nbatch, lam_init=decode["lam_init"], inline=inline)
    return in_specs, args, out_spec, out_shape, scratch, static


def _ffn_kernel(*refs, nj, emit_norm, dec):
    (x_ref, g_ref, wg_ref, wu_ref, wd_ref, gn_ref), outs, (h_ref,), side = _split_refs(
        refs, 6, 2 if emit_norm else 1, dec)
    o_ref = outs[0]
    j = pl.program_id(1)

    @pl.when(j == 0)
    def _():
        x = x_ref[...]
        h_ref[...] = _rms(x, g_ref[...]).astype(bf16)
        o_ref[...] = x

    if dec is not None:
        _ride_along(dec, side, pl.program_id(0) * nj + j)

    h = h_ref[...]
    gg = jnp.dot(h, wg_ref[...], preferred_element_type=f32)
    uu = jnp.dot(h, wu_ref[...], preferred_element_type=f32)
    act = (gg * jax.nn.sigmoid(gg)) * (uu * 0.5)
    o_ref[...] += jnp.dot(act.astype(bf16), wd_ref[...], preferred_element_type=f32)

    if emit_norm:
        @pl.when(j == nj - 1)
        def _():
            outs[1][...] = _rms(o_ref[...], gn_ref[...]).astype(bf16)


def _ffn_grid(m, ff):
    return m // _pick(m, FFN_BLOCK_ROWS), ff // _pick(ff, FFN_BLOCK_HIDDEN)


def _ffn(x, g, w_up, w_down, g_next=None, *, decode=None):
    m, d = x.shape
    ff = w_down.shape[0]
    ni, nj = _ffn_grid(m, ff)
    bm, bf = m // ni, ff // nj
    emit_norm = g_next is not None
    gn = g_next if emit_norm else g
    out_shape = [jax.ShapeDtypeStruct((m, d), f32)]
    out_specs = [pl.BlockSpec((bm, d), lambda i, j, *_: (i, 0))]
    if emit_norm:
        out_shape.append(jax.ShapeDtypeStruct((m, d), bf16))
        out_specs.append(pl.BlockSpec((bm, d), lambda i, j, *_: (i, 0)))
    in_specs = [
        pl.BlockSpec((bm, d), lambda i, j, *_: (i, 0)),
        pl.BlockSpec((1, d), lambda i, j, *_: (0, 0)),
        pl.BlockSpec((d, bf), lambda i, j, *_: (0, j)),
        pl.BlockSpec((d, bf), lambda i, j, *_: (0, j + nj)),
        pl.BlockSpec((bf, d), lambda i, j, *_: (j, 0)),
        pl.BlockSpec((1, d), lambda i, j, *_: (0, 0)),
    ]
    args = [x, g, w_up, w_up, w_down, gn]
    scratch = [pltpu.VMEM((bm, d), bf16)]
    dec = None
    prefetch = []
    if decode is not None:
        d_in, d_args, d_out, d_shape, d_scratch, dec = _ride_along_operands(
            decode, ni * nj, lambda i, j: i * nj + j)
        in_specs += d_in
        args += d_args
        out_specs.append(d_out)
        out_shape.append(d_shape)
        scratch += d_scratch
        prefetch = [decode["page_table"]]
    grid_spec = pltpu.PrefetchScalarGridSpec(
        num_scalar_prefetch=len(prefetch), grid=(ni, nj), in_specs=in_specs, out_specs=out_specs,
        scratch_shapes=scratch)
    sem = ("parallel" if dec is None else "arbitrary", "arbitrary")
    res = pl.pallas_call(
        functools.partial(_ffn_kernel, nj=nj, emit_norm=emit_norm, dec=dec),
        grid_spec=grid_spec,
        out_shape=out_shape,
        compiler_params=_cparams(*sem),
        name="ffn" if dec is None else "ffn_decode",
    )(*prefetch, *args)
    return res if len(res) > 1 else res[0]


def _proj_qk_kernel(a_ref, w_ref, cos_ref, sin_ref, *o_refs, sub, out_dtypes):
    a = a_ref[...]
    cos, sin = cos_ref[...], sin_ref[...]
    for s0 in range(0, w_ref.shape[1], sub):
        z = jnp.dot(a, w_ref[:, s0:s0 + sub], preferred_element_type=f32)
        for n in range(sub // LANES):
            x = z[:, n * LANES:(n + 1) * LANES]
            inv = lax.rsqrt(jnp.mean(x * x, axis=-1, keepdims=True) + EPS)
            r = (x * cos + pltpu.roll(x, LANES // 2, axis=1) * sin) * inv
            c0 = s0 + n * LANES
            for o_ref, dt in zip(o_refs, out_dtypes):
                o_ref[:, c0:c0 + LANES] = r.astype(dt)


def _proj_plain_kernel(a_ref, w_ref, *o_refs, act, out_dtypes):
    z = jnp.dot(a_ref[...], w_ref[...], preferred_element_type=f32)
    if act == "sigmoid":
        z = jax.nn.sigmoid(z)
    for o_ref, dt in zip(o_refs, out_dtypes):
        o_ref[...] = z.astype(dt)


def _proj(hn, w_in, col0, ncols, out_dtypes, *, qk=None, act=None, bm=1024, bn=1024):
    m, d = hn.shape
    if qk is None and len(out_dtypes) == 1:
        bm = 2 * bm
    bm = _pick(m, bm)
    if qk is not None:
        bm = _pick(qk[0].shape[0], bm)
    bn = _pick(ncols, bn)
    while col0 % bn:
        bn //= 2
    joff = col0 // bn
    in_specs = [
        pl.BlockSpec((bm, d), lambda i, j: (i, 0)),
        pl.BlockSpec((d, bn), lambda i, j: (0, j + joff)),
    ]
    args = [hn, w_in]
    if qk is not None:
        cos, sin = qk
        npos = cos.shape[0] // bm
        in_specs += [
            pl.BlockSpec((bm, LANES), lambda i, j: (i % npos, 0)),
            pl.BlockSpec((bm, LANES), lambda i, j: (i % npos, 0)),
        ]
        args += [cos, sin]
        body = functools.partial(_proj_qk_kernel, sub=_pick(bn, 2 * LANES), out_dtypes=out_dtypes)
    else:
        body = functools.partial(_proj_plain_kernel, act=act, out_dtypes=out_dtypes)
    return pl.pallas_call(
        body,
        grid=(m // bm, ncols // bn),
        in_specs=in_specs,
        out_specs=[pl.BlockSpec((bm, bn), lambda i, j: (i, j)) for _ in out_dtypes],
        out_shape=[jax.ShapeDtypeStruct((m, ncols), dt) for dt in out_dtypes],
        compiler_params=_cparams("parallel", "arbitrary"),
        name="proj",
    )(*args)


def _lambda(lam_ref, lam_init):
    r = lam_ref[...]
    s1 = jnp.sum(r[0:1] * r[1:2], axis=-1, keepdims=True)
    s2 = jnp.sum(r[2:3] * r[3:4], axis=-1, keepdims=True)
    return jnp.exp(s1) - jnp.exp(s2) + lam_init


def _attn_prompt_kernel(*refs, tk, hd, lam_init, dec):
    ((lam_ref, gsub_ref, q_ref, k_ref, v_ref), (o_ref,), (m_ref, l_ref, acc_ref),
     side) = _split_refs(refs, 5, 1, dec)
    qi = pl.program_id(2)
    vd = v_ref.shape[1]
    nparts = q_ref.shape[0] // tk
    nrep_s = tk // LANES
    nrep_v = vd // LANES

    m_ref[...] = jnp.full(m_ref.shape, -jnp.inf, f32)
    l_ref[...] = jnp.zeros(l_ref.shape, f32)
    acc_ref[...] = jnp.zeros(acc_ref.shape, f32)

    def kv_block(j, parts):
        r0 = pl.multiple_of(j * tk, tk)
        v = v_ref[pl.ds(r0, tk), :]
        for c in range(2):
            k = k_ref[pl.ds(r0, tk), c * hd:(c + 1) * hd]
            for part, diagonal in parts:
                rows = slice(part * tk, (part + 1) * tk)
                q = q_ref[rows, c * hd:(c + 1) * hd]
                s = lax.dot_general(q, k, (((1,), (1,)), ((), ())), preferred_element_type=f32)
                if diagonal:
                    row = lax.broadcasted_iota(jnp.int32, s.shape, 0)
                    col = lax.broadcasted_iota(jnp.int32, s.shape, 1)
                    s = jnp.where(row >= col, s, -jnp.inf)
                m_prev = m_ref[c, rows]
                m_new = jnp.maximum(m_prev, jnp.max(s, axis=1, keepdims=True))
                alpha = jnp.exp2(m_prev - m_new)
                p = jnp.exp2(s - _lane_repeat(m_new, nrep_s))
                l_ref[c, rows] = alpha * l_ref[c, rows] + jnp.sum(p, axis=1, keepdims=True)
                acc_ref[c, rows] = (acc_ref[c, rows] * _lane_repeat(alpha, nrep_v)
                                    + jnp.dot(p.astype(bf16), v, preferred_element_type=f32))
                m_ref[c, rows] = m_new

    def body(j, carry):
        kv_block(j, [(part, False) for part in range(nparts)])
        return carry

    lax.fori_loop(0, nparts * qi, body, 0)
    if dec is not None:
        step = ((pl.program_id(0) * pl.num_programs(1) + pl.program_id(1)) * pl.num_programs(2) + qi)
        _ride_along(dec, side, step)
    for d in range(nparts):
        kv_block(nparts * qi + d, [(part, part == d) for part in range(d, nparts)])
    o1 = acc_ref[0] / _lane_repeat(l_ref[0], nrep_v)
    o2 = acc_ref[1] / _lane_repeat(l_ref[1], nrep_v)
    d = o1 - _lambda(lam_ref, lam_init) * o2
    o_ref[...] = (_rms(d, gsub_ref[...]) * (1.0 - lam_init)).astype(o_ref.dtype)


def _attn_grid(batch, seq, n_heads):
    return batch, n_heads, seq // _pick(seq, ATTN_BLOCK_Q)


def _attn_prompt(q, k, v, lam4, g_sub, *, batch, seq, n_heads, hd, vd, lam_init, decode=None):
    _, _, nq = _attn_grid(batch, seq, n_heads)
    tq = seq // nq
    tk = _pick(tq, ATTN_BLOCK_K)
    in_specs = [
        pl.BlockSpec((4, hd), lambda b, h, i, *_: (0, 0)),
        pl.BlockSpec((1, vd), lambda b, h, i, *_: (0, 0)),
        pl.BlockSpec((tq, 2 * hd), lambda b, h, i, *_: (b * nq + i, h)),
        pl.BlockSpec((seq, 2 * hd), lambda b, h, i, *_: (b, h)),
        pl.BlockSpec((seq, vd), lambda b, h, i, *_: (b, h)),
    ]
    args = [lam4, g_sub, q, k, v]
    out_specs = [pl.BlockSpec((tq, vd), lambda b, h, i, *_: (b * nq + i, h))]
    out_shape = [jax.ShapeDtypeStruct((batch * seq, n_heads * vd), bf16)]
    scratch = [
        pltpu.VMEM((2, tq, LANES), f32),
        pltpu.VMEM((2, tq, LANES), f32),
        pltpu.VMEM((2, tq, vd), f32),
    ]
    dec = None
    prefetch = []
    if decode is not None:
        d_in, d_args, d_out, d_shape, d_scratch, dec = _ride_along_operands(
            decode, batch * n_heads * nq, lambda b, h, i: (b * n_heads + h) * nq + i)
        in_specs += d_in
        args += d_args
        out_specs.append(d_out)
        out_shape.append(d_shape)
        scratch += d_scratch
        prefetch = [decode["page_table"]]
    par = "parallel" if dec is None else "arbitrary"
    res = pl.pallas_call(
        functools.partial(_attn_prompt_kernel, tk=tk, hd=hd, lam_init=lam_init, dec=dec),
        grid_spec=pltpu.PrefetchScalarGridSpec(
            num_scalar_prefetch=len(prefetch), grid=(batch, n_heads, nq), in_specs=in_specs,
            out_specs=out_specs, scratch_shapes=scratch),
        out_shape=out_shape,
        compiler_params=_cparams(par, par, "arbitrary"),
        name="attn_prompt" if dec is None else "attn_decode_ride",
    )(*prefetch, *args)
    return res if len(res) > 1 else res[0]


def _decode_step(lam_ref, gsub_ref, q_ref, kn_ref, vn_ref, k_refs, v_refs, o_ref,
                 m_ref, l_ref, acc_ref, *, first, active, lam_init):
    nh, w = q_ref.shape[1], q_ref.shape[2]
    hd = w // 2
    rows = k_refs[0].shape[2]
    nrep_w = w // LANES

    q = q_ref[0]
    lane = lax.broadcasted_iota(jnp.int32, (nh, w), 1)
    qm = jnp.concatenate([jnp.where(lane < hd, q, 0.0), jnp.where(lane >= hd, q, 0.0)], axis=0)
    r_i = lax.broadcasted_iota(jnp.int32, (2 * nh, rows), 0)
    c_i = lax.broadcasted_iota(jnp.int32, (2 * nh, rows), 1)
    valid = (c_i % nh) == jnp.where(active, r_i % nh, -1)
    scores = []
    for k_ref in k_refs:
        s = lax.dot_general(qm, k_ref[0, 0], (((1,), (1,)), ((), ())), preferred_element_type=f32)
        scores.append(jnp.where(valid, s, -jnp.inf))
    mc = scores[0]
    for s in scores[1:]:
        mc = jnp.maximum(mc, s)
    m_prev = jnp.where(first, -jnp.inf, m_ref[...])
    l_prev = jnp.where(first, 0.0, l_ref[...])
    acc_prev = jnp.where(first, 0.0, acc_ref[...])
    m_new = jnp.maximum(m_prev, jnp.max(mc, axis=1, keepdims=True))
    alpha = jnp.exp2(m_prev - m_new)
    m_b = _lane_repeat(m_new, rows // LANES)
    lsum = None
    pv = None
    for s, v_ref in zip(scores, v_refs):
        p = jnp.exp2(s - m_b)
        ls = jnp.sum(p, axis=1, keepdims=True)
        d = jnp.dot(p, v_ref[0, 0], preferred_element_type=f32)
        lsum = ls if lsum is None else lsum + ls
        pv = d if pv is None else pv + d
    l_new = alpha * l_prev + lsum
    acc_new = _lane_repeat(alpha, nrep_w) * acc_prev + pv
    m_ref[...] = m_new
    l_ref[...] = l_new
    acc_ref[...] = acc_new

    kn = jnp.concatenate([kn_ref[0], kn_ref[0]], axis=0)
    vn = jnp.concatenate([vn_ref[0], vn_ref[0]], axis=0)
    s_self = jnp.sum(qm * kn, axis=1, keepdims=True)
    m_f = jnp.maximum(m_new, s_self)
    a_f = jnp.exp2(m_new - m_f)
    p_f = jnp.exp2(s_self - m_f)
    l_f = a_f * l_new + p_f
    acc_f = _lane_repeat(a_f, nrep_w) * acc_new + _lane_repeat(p_f, nrep_w) * vn
    o = acc_f / _lane_repeat(l_f, nrep_w)
    d = o[0:nh] - _lambda(lam_ref, lam_init) * o[nh:2 * nh]
    o_ref[0] = (_rms(d, gsub_ref[...]) * (1.0 - lam_init)).astype(o_ref.dtype)


def _attn_decode_kernel(pt_ref, lam_ref, gsub_ref, q_ref, kn_ref, vn_ref, *rest, pp, lam_init):
    k_refs, v_refs = rest[:pp], rest[pp:2 * pp]
    o_ref, m_ref, l_ref, acc_ref = rest[2 * pp:]
    g = pl.program_id(1)

    @pl.when(jnp.logical_and(pl.program_id(0) == 0, g == 0))
    def _():
        m_ref[...] = jnp.full(m_ref.shape, -jnp.inf, f32)
        l_ref[...] = jnp.zeros(l_ref.shape, f32)
        acc_ref[...] = jnp.zeros(acc_ref.shape, f32)

    _decode_step(lam_ref, gsub_ref, q_ref, kn_ref, vn_ref, k_refs, v_refs, o_ref,
                 m_ref, l_ref, acc_ref, first=g == 0, active=True, lam_init=lam_init)


def _attn_decode(decode, pp=8):
    q, cache_k, cache_v, page_table = (decode[k] for k in ("q", "cache_k", "cache_v", "page_table"))
    layer, batch0, nbatch = decode["layer"], decode["batch0"], decode["nbatch"]
    _, nh, w = q.shape
    hd = w // 2
    n_pages = page_table.shape[1]
    pp = _pick(n_pages, pp)
    page_rows = cache_k.shape[2]

    def page_spec(pi):
        return pl.BlockSpec((1, 1, page_rows, w),
                            lambda i, g, pt, pi=pi: (layer, pt[batch0 + i, g * pp + pi], 0, 0))

    row_in = pl.BlockSpec((1, nh, w), lambda i, g, pt: (batch0 + i, 0, 0))
    grid_spec = pltpu.PrefetchScalarGridSpec(
        num_scalar_prefetch=1,
        grid=(nbatch, n_pages // pp),
        in_specs=[
            pl.BlockSpec((4, hd), lambda i, g, pt: (0, 0)),
            pl.BlockSpec((1, w), lambda i, g, pt: (0, 0)),
            row_in, row_in, row_in,
        ] + [page_spec(pi) for pi in range(pp)] * 2,
        out_specs=pl.BlockSpec((1, nh, w), lambda i, g, pt: (i, 0, 0)),
        scratch_shapes=[
            pltpu.VMEM((2 * nh, LANES), f32),
            pltpu.VMEM((2 * nh, LANES), f32),
            pltpu.VMEM((2 * nh, w), f32),
        ],
    )
    return pl.pallas_call(
        functools.partial(_attn_decode_kernel, pp=pp, lam_init=decode["lam_init"]),
        grid_spec=grid_spec,
        out_shape=jax.ShapeDtypeStruct((nbatch, nh, w), bf16),
        compiler_params=_cparams("arbitrary", "arbitrary"),
        name="attn_decode",
    )(page_table, decode["lam4"], decode["g_sub"], q, decode["k_new"], decode["v_new"],
      *([cache_k] * pp), *([cache_v] * pp))


def _lru_gates(xc, wa_ref, ba_ref, wx_ref, bx_ref, ap_ref):
    nb, blk = wa_ref.shape[0], wa_ref.shape[1]
    xcb = xc.astype(bf16)
    za = jnp.concatenate(
        [jnp.dot(xcb[:, n * blk:(n + 1) * blk], wa_ref[n], preferred_element_type=f32)
         for n in range(nb)], axis=1)
    zx = jnp.concatenate(
        [jnp.dot(xcb[:, n * blk:(n + 1) * blk], wx_ref[n], preferred_element_type=f32)
         for n in range(nb)], axis=1)
    gate_a = jax.nn.sigmoid(za + ba_ref[...])
    gate_x = jax.nn.sigmoid(zx + bx_ref[...])
    ap = -ap_ref[...]
    softplus = jnp.maximum(ap, 0.0) + jnp.log1p(jnp.exp(-jnp.abs(ap)))
    log_a = (-LRU_C) * gate_a * softplus
    a = jnp.exp(log_a)
    mult = jnp.sqrt(-jnp.tanh(log_a) * (a * a + 1.0))
    return a, gate_x, mult


def _lru_prompt_kernel(*refs, tt, pos0, dec):
    ((xr_ref, xg_ref, cw_ref, cb_ref, wa_ref, ba_ref, wx_ref, bx_ref, ap_ref),
     (o_ref, conv_ref, h_ref), (xprev_ref, hc_ref, a_s, b_s), side) = _split_refs(refs, 9, 3, dec)
    t = pl.program_id(1)
    nt = pl.num_programs(1)
    w = xr_ref.shape[1]
    cwid = cw_ref.shape[0]
    assert cwid <= SUBLANES + 1
    ng = tt // SUBLANES

    @pl.when(t == 0)
    def _():
        xprev_ref[...] = jnp.zeros(xprev_ref.shape, f32)
        hc_ref[...] = jnp.zeros(hc_ref.shape, f32)

    if dec is not None:
        _ride_along(dec, side, pl.program_id(0) * nt + t)

    x = xr_ref[...].reshape(ng, SUBLANES, w)
    prev = xprev_ref[...]
    sub = lax.broadcasted_iota(jnp.int32, (ng, SUBLANES, w), 1)
    xc = x * cw_ref[cwid - 1:cwid, :]
    for i in range(1, cwid):
        r = pltpu.roll(x, i, axis=1)
        r_prev = jnp.concatenate([pltpu.roll(prev, i, axis=0)[None], r[:ng - 1]], axis=0)
        xc = xc + jnp.where(sub >= i, r, r_prev) * cw_ref[cwid - 1 - i:cwid - i, :]
    xc = (xc + cb_ref[...]).reshape(tt, w)
    xprev_ref[...] = x[ng - 1]

    a, gate_x, mult = _lru_gates(xc, wa_ref, ba_ref, wx_ref, bx_ref, ap_ref)
    if pos0 == 0:
        row = lax.broadcasted_iota(jnp.int32, (tt, w), 0)
        mult = jnp.where((row + t * tt) == 0, 1.0, mult)
    b = xc * gate_x * mult

    a = a.reshape(ng, SUBLANES, w)
    b = b.reshape(ng, SUBLANES, w)
    for s in (1, 2, 4):
        keep = sub >= s
        b = jnp.where(keep, a * pltpu.roll(b, s, axis=1) + b, b)
        a = jnp.where(keep, a * pltpu.roll(a, s, axis=1), a)
    a_s[...] = a.reshape(tt, w)
    b_s[...] = b.reshape(tt, w)

    def grp(gi, h):
        r0 = pl.multiple_of(gi * SUBLANES, SUBLANES)
        hg = a_s[pl.ds(r0, SUBLANES), :] * h + b_s[pl.ds(r0, SUBLANES), :]
        b_s[pl.ds(r0, SUBLANES), :] = hg
        return jnp.broadcast_to(hg[SUBLANES - 1:SUBLANES, :], (SUBLANES, w))

    h = lax.fori_loop(0, tt // SUBLANES, grp, hc_ref[...])
    hc_ref[...] = h
    o_ref[...] = (b_s[...] * _gelu_tanh(xg_ref[...].astype(f32))).astype(o_ref.dtype)

    @pl.when(t == nt - 1)
    def _():
        conv_ref[0] = xprev_ref[SUBLANES - (cwid - 1):SUBLANES, :]
        h_ref[0] = h[0:1, :]


def _lru_grid(batch, seq):
    return batch, seq // _pick(seq, LRU_TILE_ROWS)


def _lru_prompt(xr, xg, conv_w, conv_b, w_a, b_a, w_x, b_x, a_param, *, batch, seq, pos0,
                decode=None):
    w = xr.shape[1]
    _, nt = _lru_grid(batch, seq)
    tt = seq // nt
    cwid = conv_w.shape[0]
    vec = pl.BlockSpec((1, w), lambda b, t, *_: (0, 0))
    wspec = pl.BlockSpec(w_a.shape, lambda b, t, *_: (0, 0, 0))
    in_specs = [
        pl.BlockSpec((tt, w), lambda b, t, *_: (b * nt + t, 0)),
        pl.BlockSpec((tt, w), lambda b, t, *_: (b * nt + t, 0)),
        pl.BlockSpec((cwid, w), lambda b, t, *_: (0, 0)),
        vec, wspec, vec, wspec, vec, vec,
    ]
    args = [xr, xg, conv_w, conv_b, w_a, b_a, w_x, b_x, a_param]
    out_specs = [
        pl.BlockSpec((tt, w), lambda b, t, *_: (b * nt + t, 0)),
        pl.BlockSpec((1, cwid - 1, w), lambda b, t, *_: (b, 0, 0)),
        pl.BlockSpec((1, 1, w), lambda b, t, *_: (b, 0, 0)),
    ]
    out_shape = [
        jax.ShapeDtypeStruct((batch * seq, w), bf16),
        jax.ShapeDtypeStruct((batch, cwid - 1, w), f32),
        jax.ShapeDtypeStruct((batch, 1, w), f32),
    ]
    scratch = [
        pltpu.VMEM((SUBLANES, w), f32),
        pltpu.VMEM((SUBLANES, w), f32),
        pltpu.VMEM((tt, w), f32),
        pltpu.VMEM((tt, w), f32),
    ]
    dec = None
    prefetch = []
    if decode is not None:
        d_in, d_args, d_out, d_shape, d_scratch, dec = _ride_along_operands(
            decode, batch * nt, lambda b, t: b * nt + t)
        in_specs += d_in
        args += d_args
        out_specs.append(d_out)
        out_shape.append(d_shape)
        scratch += d_scratch
        prefetch = [decode["page_table"]]
    return pl.pallas_call(
        functools.partial(_lru_prompt_kernel, tt=tt, pos0=pos0, dec=dec),
        grid_spec=pltpu.PrefetchScalarGridSpec(
            num_scalar_prefetch=len(prefetch), grid=(batch, nt), in_specs=in_specs,
            out_specs=out_specs, scratch_shapes=scratch),
        out_shape=out_shape,
        compiler_params=pltpu.CompilerParams(
            dimension_semantics=("parallel" if dec is None else "arbitrary", "arbitrary"),
            vmem_limit_bytes=LRU_VMEM_LIMIT_BYTES),
        name="lru_prompt" if dec is None else "lru_decode",
    )(*prefetch, *args)


def _lru_step_kernel(xr_ref, xg_ref, buf_ref, h0_ref, cw_ref, cb_ref, wa_ref, ba_ref, wx_ref,
                     bx_ref, ap_ref, o_ref, conv_ref, h_ref, *, pos0):
    cwid = cw_ref.shape[0]
    x = xr_ref[...]
    xc = cb_ref[...] + x * cw_ref[cwid - 1:cwid, :]
    for i in range(cwid - 1):
        xc = xc + buf_ref[i] * cw_ref[i:i + 1, :]
        if i > 0:
            conv_ref[i - 1] = buf_ref[i]
    conv_ref[cwid - 2] = x
    a, gate_x, mult = _lru_gates(xc, wa_ref, ba_ref, wx_ref, bx_ref, ap_ref)
    if pos0 == 0:
        mult = jnp.ones_like(mult)
    h = a * h0_ref[...] + xc * gate_x * mult
    h_ref[...] = h
    o_ref[...] = (h * _gelu_tanh(xg_ref[...].astype(f32))).astype(o_ref.dtype)


def _lru_step(xr, xg, buf, h0, conv_w, conv_b, w_a, b_a, w_x, b_x, a_param, *, pos0):
    b, w = xr.shape
    cwid = conv_w.shape[0]
    vec = pl.BlockSpec((1, w), lambda i: (0, 0))
    wspec = pl.BlockSpec(w_a.shape, lambda i: (0, 0, 0))
    return pl.pallas_call(
        functools.partial(_lru_step_kernel, pos0=pos0),
        grid=(1,),
        in_specs=[
            pl.BlockSpec((b, w), lambda i: (0, 0)),
            pl.BlockSpec((b, w), lambda i: (0, 0)),
            pl.BlockSpec((cwid - 1, b, w), lambda i: (0, 0, 0)),
            pl.BlockSpec((b, w), lambda i: (0, 0)),
            pl.BlockSpec((cwid, w), lambda i: (0, 0)),
            vec, wspec, vec, wspec, vec, vec,
        ],
        out_specs=[
            pl.BlockSpec((b, w), lambda i: (0, 0)),
            pl.BlockSpec((cwid - 1, b, w), lambda i: (0, 0, 0)),
            pl.BlockSpec((b, w), lambda i: (0, 0)),
        ],
        out_shape=[
            jax.ShapeDtypeStruct((b, w), bf16),
            jax.ShapeDtypeStruct((cwid - 1, b, w), f32),
            jax.ShapeDtypeStruct((b, w), f32),
        ],
        compiler_params=_cparams("arbitrary"),
        name="lru_step",
    )(xr, xg, buf, h0, conv_w, conv_b, w_a, b_a, w_x, b_x, a_param)


def _merge_kernel(a_ref, b_ref, wpa_ref, wpb_ref, ga_ref, gb_ref, wo_ref, x_ref, o_ref):
    j = pl.program_id(1)

    @pl.when(j == 0)
    def _():
        o_ref[...] = x_ref[...]

    ya = jnp.dot(a_ref[...], wpa_ref[...], preferred_element_type=f32)
    yb = jnp.dot(b_ref[...], wpb_ref[...], preferred_element_type=f32)
    y = ga_ref[...].astype(f32) * ya + gb_ref[...].astype(f32) * yb
    o_ref[...] += jnp.dot(y.astype(bf16), wo_ref[...], preferred_element_type=f32)


def _merge(a, b, w_pa, w_pb, gates, w_o, x, *, bm=512, bn=512):
    m, d = x.shape
    ka = a.shape[1]
    kb = b.shape[1]
    bm = _pick(m, bm)
    bn = _pick(d, bn)
    nj = d // bn
    return pl.pallas_call(
        _merge_kernel,
        grid=(m // bm, nj),
        in_specs=[
            pl.BlockSpec((bm, ka), lambda i, j: (i, 0)),
            pl.BlockSpec((bm, kb), lambda i, j: (i, 0)),
            pl.BlockSpec((ka, bn), lambda i, j: (0, j)),
            pl.BlockSpec((kb, bn), lambda i, j: (0, j)),
            pl.BlockSpec((bm, bn), lambda i, j: (i, j)),
            pl.BlockSpec((bm, bn), lambda i, j: (i, j + nj)),
            pl.BlockSpec((bn, d), lambda i, j: (j, 0)),
            pl.BlockSpec((bm, d), lambda i, j: (i, 0)),
        ],
        out_specs=pl.BlockSpec((bm, d), lambda i, j: (i, 0)),
        out_shape=jax.ShapeDtypeStruct((m, d), f32),
        compiler_params=_cparams("parallel", "arbitrary"),
        name="merge",
    )(a, b, w_pa, w_pb, gates, gates, w_o, x)


def _rope_tables(positions, hd):
    half = hd // 2
    inv = 1.0 / (ROPE_THETA ** (jnp.arange(half, dtype=f32) / half))
    ang = positions.astype(f32)[:, None] * inv[None, :]
    cos, sin = jnp.cos(ang), jnp.sin(ang)
    return jnp.concatenate([cos, cos], axis=-1), jnp.concatenate([-sin, sin], axis=-1)


def _lambda_init(layer):
    return 0.8 - 0.6 * math.exp(-0.3 * layer)


def kernel(x_prompt, x_sample, cache_k, cache_v, state_conv, state_h, page_table, g_ffn1, w_up1, w_down1, g_mix, w_in, g_q, g_k, lam_q1, lam_k1, lam_q2, lam_k2, g_sub, conv_w, conv_b, w_a, b_a, w_x, b_x, a_param, w_pa, w_pb, w_o, g_ffn2, w_up2, w_down2):
    batch, seq, d = x_prompt.shape
    dec_b, dec_t, _ = x_sample.shape
    assert dec_t == 1, "sample group decodes one token per sequence"
    depth, n_pool, page, n_heads, kw = cache_k.shape
    hd = g_q.shape[1]
    vd = g_sub.shape[1]
    assert kw == 2 * hd and vd == 2 * hd
    lw = conv_w.shape[2]
    qk_w = n_heads * 2 * hd
    at_w = n_heads * vd
    past = page_table.shape[1] * page
    scale = hd ** -0.5 * math.log2(math.e)

    rope_p = _rope_tables(jnp.arange(seq), hd)
    rope_s = _rope_tables(jnp.full((dec_b,), past), hd)

    def fold(rope, g, s):
        return rope[0] * (g * s), rope[1] * (jnp.roll(g, hd // 2, axis=-1) * s)

    yp = x_prompt.reshape(batch * seq, d)
    ys = x_sample.reshape(dec_b, d)
    outs = [[] for _ in range(8)]
    for l in range(depth):
        lam_init = _lambda_init(l)
        wup1, wdn1 = w_up1[l].astype(bf16), w_down1[l].astype(bf16)
        wup2, wdn2 = w_up2[l].astype(bf16), w_down2[l].astype(bf16)
        win = w_in[l].astype(bf16)
        wpa, wpb, wo = w_pa[l].astype(bf16), w_pb[l].astype(bf16), w_o[l].astype(bf16)
        wa, wx = w_a[l].astype(bf16), w_x[l].astype(bf16)
        lam4 = jnp.concatenate([lam_q1[l][None], lam_k1[l][None], lam_q2[l][None], lam_k2[l][None]], 0)
        gsub, gq, gk = g_sub[l][None], g_q[l][None], g_k[l][None]
        lru_w = (conv_w[l], conv_b[l][None], wa, b_a[l][None], wx, b_x[l][None], a_param[l][None])
        ck = cache_k.reshape(depth, n_pool, page * n_heads, kw)
        cv = cache_v.reshape(depth, n_pool, page * n_heads, vd)

        def projections(hn, rope, q_dtype):
            (q,) = _proj(hn, win, 0, qk_w, [q_dtype], qk=fold(rope, gq, scale))
            k32, k16 = _proj(hn, win, qk_w, qk_w, [f32, bf16], qk=fold(rope, gk, 1.0))
            v32, v16 = _proj(hn, win, 2 * qk_w, at_w, [f32, bf16])
            (xr,) = _proj(hn, win, 2 * qk_w + at_w, lw, [f32])
            (xg,) = _proj(hn, win, 2 * qk_w + at_w + lw, lw, [bf16])
            (gates,) = _proj(hn, win, 2 * qk_w + at_w + 2 * lw, 2 * d, [bf16], act="sigmoid")
            return q, k32, k16, v32, v16, xr, xg, gates

        x1_s, hn_s = _ffn(ys, g_ffn1[l][None], wup1, wdn1, g_mix[l][None])
        q_s, k32_s, _, v32_s, _, xr_s, xg_s, gates_s = projections(hn_s, rope_s, f32)
        n_pages = page_table.shape[1]
        dec = dict(q=q_s.reshape(dec_b, n_heads, kw), k_new=k32_s.reshape(dec_b, n_heads, kw),
                   v_new=v32_s.reshape(dec_b, n_heads, vd), cache_k=ck, cache_v=cv,
                   page_table=page_table, lam4=lam4, g_sub=gsub, layer=l, lam_init=lam_init)
        pp_lru = _pick(n_pages, DECODE_PAGES_PER_LRU_STEP)
        pp_att = _pick(n_pages, DECODE_PAGES_PER_ATTN_STEP)
        pp_ffn = _pick(n_pages, DECODE_PAGES_PER_FFN_STEP)
        lb, lt = _lru_grid(batch, seq)
        ab, ah, aq = _attn_grid(batch, seq, n_heads)
        ni, nj = _ffn_grid(batch * seq, wdn1.shape[0])
        n_lru = min(dec_b, (lb * lt) // (n_pages // pp_lru))
        n_att = min(dec_b - n_lru, (ab * ah * aq) // (n_pages // pp_att))
        fit = (ni * nj) // (n_pages // pp_ffn)
        n1 = min(fit, dec_b - n_lru - n_att)
        n2 = min(fit, dec_b - n_lru - n_att - n1)
        n3 = dec_b - n_lru - n_att - n1 - n2
        att_s = []

        def host(fn, *args, nbatch, pp, **kw):
            if not nbatch:
                return fn(*args, **kw)
            batch0 = sum(a.shape[0] for a in att_s)
            *res, att_part = fn(*args, decode=dict(dec, batch0=batch0, nbatch=nbatch, pp=pp), **kw)
            att_s.append(att_part)
            return res[0] if len(res) == 1 else res

        x1, hn = host(_ffn, yp, g_ffn1[l][None], wup1, wdn1, g_mix[l][None], nbatch=n1, pp=pp_ffn)
        q, k32, k16, v32, v16, xr, xg, gates = projections(hn, rope_p, bf16)
        att = host(_attn_prompt, q, k16, v16, lam4, gsub, nbatch=n_att, pp=pp_att, batch=batch,
                   seq=seq, n_heads=n_heads, hd=hd, vd=vd, lam_init=lam_init)
        lru, conv_p, h_p = host(_lru_prompt, xr, xg, *lru_w, nbatch=n_lru, pp=pp_lru,
                                batch=batch, seq=seq, pos0=0)
        x2 = _merge(att, lru, wpa, wpb, gates, wo, x1)
        yp = host(_ffn, x2, g_ffn2[l][None], wup2, wdn2, nbatch=n2, pp=pp_ffn)
        if n3:
            batch0 = sum(a.shape[0] for a in att_s)
            att_s.append(_attn_decode(dict(dec, batch0=batch0, nbatch=n3)))
        outs[0].append(k32.reshape(batch, seq, n_heads, kw))
        outs[1].append(v32.reshape(batch, seq, n_heads, vd))
        outs[2].append(conv_p)
        outs[3].append(h_p.reshape(batch, lw))

        att = jnp.concatenate(att_s, axis=0) if len(att_s) > 1 else att_s[0]
        lru, conv_s, h_s = _lru_step(xr_s, xg_s, jnp.moveaxis(state_conv[l], 1, 0), state_h[l],
                                     *lru_w, pos0=past)
        x2 = _merge(att.reshape(dec_b, at_w), lru, wpa, wpb, gates_s, wo, x1_s)
        ys = _ffn(x2, g_ffn2[l][None], wup2, wdn2)
        outs[4].append(k32_s.reshape(dec_b, dec_t, n_heads, kw))
        outs[5].append(v32_s.reshape(dec_b, dec_t, n_heads, vd))
        outs[6].append(jnp.moveaxis(conv_s, 0, 1))
        outs[7].append(h_s)

    return (yp.reshape(batch, seq, d), ys.reshape(dec_b, dec_t, d),
            *[jnp.stack(o) for o in outs])
```

```python
import functools
import math

import jax
import jax.numpy as jnp
from jax import lax
from jax.experimental import pallas as pl
from jax.experimental.pallas import tpu as pltpu

EPS = 1e-6
LRU_C = 8.0
ROPE_THETA = 10000.0
LANES = 128
SUBLANES = 8
VMEM_LIMIT_BYTES = 56 * 1024 * 1024
FFN_BLOCK_ROWS = 512
FFN_BLOCK_HIDDEN = 512
DECODE_PAGES_PER_FFN_STEP = 4
LRU_TILE_ROWS = 128
DECODE_PAGES_PER_LRU_STEP = 8
ATTN_BLOCK_Q = 1024
ATTN_BLOCK_K = 512
DECODE_PAGES_PER_ATTN_STEP = 8
LRU_VMEM_LIMIT_BYTES = 60 * 1024 * 1024

f32 = jnp.float32
bf16 = jnp.bfloat16


def _cparams(*sem):
    return pltpu.CompilerParams(dimension_semantics=sem, vmem_limit_bytes=VMEM_LIMIT_BYTES)


def _rms(x, g):
    return x * lax.rsqrt(jnp.mean(x * x, axis=-1, keepdims=True) + EPS) * g


def _lane_repeat(x, n):
    return x if n == 1 else jnp.concatenate([x] * n, axis=1)


def _gelu_tanh(x):
    c = math.sqrt(2.0 / math.pi)
    return 0.5 * x * (1.0 + jnp.tanh(c * (x + 0.044715 * (x * x * x))))


def _pick(n, pref):
    if n <= pref:
        return n
    b = pref
    while n % b:
        b //= 2
    return b


def _split_refs(refs, n_in, n_out, dec):
    if dec is None:
        return refs[:n_in], refs[n_in:n_in + n_out], refs[n_in + n_out:], None
    refs = refs[1:]
    nd = 5 + 2 * dec["pp"]
    o0 = n_in + nd
    return (refs[:n_in], refs[o0:o0 + n_out], refs[o0 + n_out + 1:-3],
            (refs[n_in:o0], refs[o0 + n_out], refs[-3:]))


def _ride_along(dec, side, step):
    ins, att_ref, (m_ref, l_ref, acc_ref) = side
    pp = dec["pp"]

    @pl.when(step == 0)
    def _():
        m_ref[...] = jnp.full(m_ref.shape, -jnp.inf, f32)
        l_ref[...] = jnp.zeros(l_ref.shape, f32)
        acc_ref[...] = jnp.zeros(acc_ref.shape, f32)

    def run(first, active):
        _decode_step(*ins[:5], ins[5:5 + pp], ins[5 + pp:], att_ref, m_ref, l_ref, acc_ref,
                     first=first, active=active, lam_init=dec["lam_init"])

    active = step // dec["spb"] < dec["nbatch"]
    if dec["inline"]:
        run(jnp.logical_and(active, step % dec["spb"] == 0), active)
    else:
        pl.when(active)(lambda: run(step % dec["spb"] == 0, True))


def _ride_along_operands(decode, nsteps, step_of):
    q, ck, cv, pt = decode["q"], decode["cache_k"], decode["cache_v"], decode["page_table"]
    _, nh, w = q.shape
    hd = w // 2
    layer, batch0, nbatch, pp = decode["layer"], decode["batch0"], decode["nbatch"], decode["pp"]
    page_rows = ck.shape[2]
    spb = pt.shape[1] // pp
    assert pt.shape[1] % pp == 0 and 0 < nbatch * spb <= nsteps

    def seq_of(*idx):
        return jnp.minimum(step_of(*idx) // spb, nbatch - 1)

    def page_spec(pi):
        def imap(*idx_pt):
            idx, pt_ref = idx_pt[:-1], idx_pt[-1]
            s = step_of(*idx)
            grp = jnp.where(s // spb < nbatch, s % spb, spb - 1)
            return (layer, pt_ref[batch0 + seq_of(*idx), grp * pp + pi], 0, 0)
        return pl.BlockSpec((1, 1, page_rows, w), imap)

    row_in = pl.BlockSpec((1, nh, w), lambda *a: (batch0 + seq_of(*a[:-1]), 0, 0))
    in_specs = [
        pl.BlockSpec((4, hd), lambda *a: (0, 0)),
        pl.BlockSpec((1, w), lambda *a: (0, 0)),
        row_in, row_in, row_in,
    ] + [page_spec(pi) for pi in range(pp)] * 2
    args = [decode["lam4"], decode["g_sub"], q, decode["k_new"], decode["v_new"]]
    args += [ck] * pp + [cv] * pp
    out_spec = pl.BlockSpec((1, nh, w), lambda *a: (seq_of(*a[:-1]), 0, 0))
    out_shape = jax.ShapeDtypeStruct((nbatch, nh, w), bf16)
    scratch = [pltpu.VMEM((2 * nh, LANES), f32), pltpu.VMEM((2 * nh, LANES), f32),
               pltpu.VMEM((2 * nh, w), f32)]
    inline = 10 * nbatch * spb >= 9 * nsteps
    static = dict(pp=pp, spb=spb, nbatch=nbatch, lam_init=decode["lam_init"], inline=inline)
    return in_specs, args, out_spec, out_shape, scratch, static


def _ffn_kernel(*refs, nj, emit_norm, dec):
    (x_ref, g_ref, wg_ref, wu_ref, wd_ref, gn_ref), outs, (h_ref,), side = _split_refs(
        refs, 6, 2 if emit_norm else 1, dec)
    o_ref = outs[0]
    j = pl.program_id(1)

    @pl.when(j == 0)
    def _():
        x = x_ref[...]
        h_ref[...] = _rms(x, g_ref[...]).astype(bf16)
        o_ref[...] = x

    if dec is not None:
        _ride_along(dec, side, pl.program_id(0) * nj + j)

    h = h_ref[...]
    gg = jnp.dot(h, wg_ref[...], preferred_element_type=f32)
    uu = jnp.dot(h, wu_ref[...], preferred_element_type=f32)
    act = (gg * jax.nn.sigmoid(gg)) * (uu * 0.5)
    o_ref[...] += jnp.dot(act.astype(bf16), wd_ref[...], preferred_element_type=f32)

    if emit_norm:
        @pl.when(j == nj - 1)
        def _():
            outs[1][...] = _rms(o_ref[...], gn_ref[...]).astype(bf16)


def _ffn_grid(m, ff):
    return m // _pick(m, FFN_BLOCK_ROWS), ff // _pick(ff, FFN_BLOCK_HIDDEN)


def _ffn(x, g, w_up, w_down, g_next=None, *, decode=None):
    m, d = x.shape
    ff = w_down.shape[0]
    ni, nj = _ffn_grid(m, ff)
    bm, bf = m // ni, ff // nj
    emit_norm = g_next is not None
    gn = g_next if emit_norm else g
    out_shape = [jax.ShapeDtypeStruct((m, d), f32)]
    out_specs = [pl.BlockSpec((bm, d), lambda i, j, *_: (i, 0))]
    if emit_norm:
        out_shape.append(jax.ShapeDtypeStruct((m, d), bf16))
        out_specs.append(pl.BlockSpec((bm, d), lambda i, j, *_: (i, 0)))
    in_specs = [
        pl.BlockSpec((bm, d), lambda i, j, *_: (i, 0)),
        pl.BlockSpec((1, d), lambda i, j, *_: (0, 0)),
        pl.BlockSpec((d, bf), lambda i, j, *_: (0, j)),
        pl.BlockSpec((d, bf), lambda i, j, *_: (0, j + nj)),
        pl.BlockSpec((bf, d), lambda i, j, *_: (j, 0)),
        pl.BlockSpec((1, d), lambda i, j, *_: (0, 0)),
    ]
    args = [x, g, w_up, w_up, w_down, gn]
    scratch = [pltpu.VMEM((bm, d), bf16)]
    dec = None
    prefetch = []
    if decode is not None:
        d_in, d_args, d_out, d_shape, d_scratch, dec = _ride_along_operands(
            decode, ni * nj, lambda i, j: i * nj + j)
        in_specs += d_in
        args += d_args
        out_specs.append(d_out)
        out_shape.append(d_shape)
        scratch += d_scratch
        prefetch = [decode["page_table"]]
    grid_spec = pltpu.PrefetchScalarGridSpec(
        num_scalar_prefetch=len(prefetch), grid=(ni, nj), in_specs=in_specs, out_specs=out_specs,
        scratch_shapes=scratch)
    sem = ("parallel" if dec is None else "arbitrary", "arbitrary")
    res = pl.pallas_call(
        functools.partial(_ffn_kernel, nj=nj, emit_norm=emit_norm, dec=dec),
        grid_spec=grid_spec,
        out_shape=out_shape,
        compiler_params=_cparams(*sem),
        name="ffn" if dec is None else "ffn_decode",
    )(*prefetch, *args)
    return res if len(res) > 1 else res[0]


def _stationary_weights(w_ref, wb_ref):
    @pl.when(pl.program_id(1) == 0)
    def _():
        wb_ref[...] = w_ref[...].astype(bf16)


def _proj_qk_kernel(a_ref, w_ref, cos_ref, sin_ref, *rest, sub, out_dtypes):
    o_refs, wb_ref = rest[:-1], rest[-1]
    _stationary_weights(w_ref, wb_ref)
    a = a_ref[...]
    cos, sin = cos_ref[...], sin_ref[...]
    for s0 in range(0, w_ref.shape[1], sub):
        z = jnp.dot(a, wb_ref[:, s0:s0 + sub], preferred_element_type=f32)
        for n in range(sub // LANES):
            x = z[:, n * LANES:(n + 1) * LANES]
            inv = lax.rsqrt(jnp.mean(x * x, axis=-1, keepdims=True) + EPS)
            r = (x * cos + pltpu.roll(x, LANES // 2, axis=1) * sin) * inv
            c0 = s0 + n * LANES
            for o_ref, dt in zip(o_refs, out_dtypes):
                o_ref[:, c0:c0 + LANES] = r.astype(dt)


def _proj_plain_kernel(a_ref, w_ref, *rest, act, out_dtypes):
    o_refs, wb_ref = rest[:-1], rest[-1]
    _stationary_weights(w_ref, wb_ref)
    z = jnp.dot(a_ref[...], wb_ref[...], preferred_element_type=f32)
    if act == "sigmoid":
        z = jax.nn.sigmoid(z)
    for o_ref, dt in zip(o_refs, out_dtypes):
        o_ref[...] = z.astype(dt)


def _proj(hn, w_in, col0, ncols, out_dtypes, *, qk=None, act=None, bm=1024, bn=1024):
    m, d = hn.shape
    bm = _pick(m, bm)
    if qk is not None:
        bm = _pick(qk[0].shape[0], bm)
    bn = _pick(ncols, bn)
    while col0 % bn:
        bn //= 2
    joff = col0 // bn
    in_specs = [
        pl.BlockSpec((bm, d), lambda j, i: (i, 0)),
        pl.BlockSpec((d, bn), lambda j, i: (0, j + joff)),
    ]
    args = [hn, w_in]
    if qk is not None:
        cos, sin = qk
        npos = cos.shape[0] // bm
        in_specs += [
            pl.BlockSpec((bm, LANES), lambda j, i: (i % npos, 0)),
            pl.BlockSpec((bm, LANES), lambda j, i: (i % npos, 0)),
        ]
        args += [cos, sin]
        body = functools.partial(_proj_qk_kernel, sub=_pick(bn, 2 * LANES), out_dtypes=out_dtypes)
    else:
        body = functools.partial(_proj_plain_kernel, act=act, out_dtypes=out_dtypes)
    return pl.pallas_call(
        body,
        grid=(ncols // bn, m // bm),
        in_specs=in_specs,
        out_specs=[pl.BlockSpec((bm, bn), lambda j, i: (i, j)) for _ in out_dtypes],
        out_shape=[jax.ShapeDtypeStruct((m, ncols), dt) for dt in out_dtypes],
        scratch_shapes=[pltpu.VMEM((d, bn), bf16)],
        compiler_params=_cparams("parallel", "arbitrary"),
        name="proj",
    )(*args)


def _lambda(lam_ref, lam_init):
    r = lam_ref[...]
    s1 = jnp.sum(r[0:1] * r[1:2], axis=-1, keepdims=True)
    s2 = jnp.sum(r[2:3] * r[3:4], axis=-1, keepdims=True)
    return jnp.exp(s1) - jnp.exp(s2) + lam_init


def _attn_prompt_kernel(*refs, tk, hd, lam_init, dec):
    ((lam_ref, gsub_ref, q_ref, k_ref, v_ref), (o_ref,), (m_ref, l_ref, acc_ref),
     side) = _split_refs(refs, 5, 1, dec)
    qi = pl.program_id(2)
    vd = v_ref.shape[1]
    nparts = q_ref.shape[0] // tk
    nrep_s = tk // LANES
    nrep_v = vd // LANES

    m_ref[...] = jnp.full(m_ref.shape, -jnp.inf, f32)
    l_ref[...] = jnp.zeros(l_ref.shape, f32)
    acc_ref[...] = jnp.zeros(acc_ref.shape, f32)

    def kv_block(j, parts):
        r0 = pl.multiple_of(j * tk, tk)
        v = v_ref[pl.ds(r0, tk), :]
        for c in range(2):
            k = k_ref[pl.ds(r0, tk), c * hd:(c + 1) * hd]
            for part, diagonal in parts:
                rows = slice(part * tk, (part + 1) * tk)
                q = q_ref[rows, c * hd:(c + 1) * hd]
                s = lax.dot_general(q, k, (((1,), (1,)), ((), ())), preferred_element_type=f32)
                if diagonal:
                    row = lax.broadcasted_iota(jnp.int32, s.shape, 0)
                    col = lax.broadcasted_iota(jnp.int32, s.shape, 1)
                    s = jnp.where(row >= col, s, -jnp.inf)
                m_prev = m_ref[c, rows]
                m_new = jnp.maximum(m_prev, jnp.max(s, axis=1, keepdims=True))
                alpha = jnp.exp2(m_prev - m_new)
                p = jnp.exp2(s - _lane_repeat(m_new, nrep_s))
                l_ref[c, rows] = alpha * l_ref[c, rows] + jnp.sum(p, axis=1, keepdims=True)
                acc_ref[c, rows] = (acc_ref[c, rows] * _lane_repeat(alpha, nrep_v)
                                    + jnp.dot(p.astype(bf16), v, preferred_element_type=f32))
                m_ref[c, rows] = m_new

    def body(j, carry):
        kv_block(j, [(part, False) for part in range(nparts)])
        return carry

    lax.fori_loop(0, nparts * qi, body, 0)
    if dec is not None:
        step = ((pl.program_id(0) * pl.num_programs(1) + pl.program_id(1)) * pl.num_programs(2) + qi)
        _ride_along(dec, side, step)
    for d in range(nparts):
        kv_block(nparts * qi + d, [(part, part == d) for part in range(d, nparts)])
    o1 = acc_ref[0] / _lane_repeat(l_ref[0], nrep_v)
    o2 = acc_ref[1] / _lane_repeat(l_ref[1], nrep_v)
    d = o1 - _lambda(lam_ref, lam_init) * o2
    o_ref[...] = (_rms(d, gsub_ref[...]) * (1.0 - lam_init)).astype(o_ref.dtype)


def _attn_grid(batch, seq, n_heads):
    return batch, n_heads, seq // _pick(seq, ATTN_BLOCK_Q)


def _attn_prompt(q, k, v, lam4, g_sub, *, batch, seq, n_heads, hd, vd, lam_init, decode=None):
    _, _, nq = _attn_grid(batch, seq, n_heads)
    tq = seq // nq
    tk = _pick(tq, ATTN_BLOCK_K)
    in_specs = [
        pl.BlockSpec((4, hd), lambda b, h, i, *_: (0, 0)),
        pl.BlockSpec((1, vd), lambda b, h, i, *_: (0, 0)),
        pl.BlockSpec((tq, 2 * hd), lambda b, h, i, *_: (b * nq + i, h)),
        pl.BlockSpec((seq, 2 * hd), lambda b, h, i, *_: (b, h)),
        pl.BlockSpec((seq, vd), lambda b, h, i, *_: (b, h)),
    ]
    args = [lam4, g_sub, q, k, v]
    out_specs = [pl.BlockSpec((tq, vd), lambda b, h, i, *_: (b * nq + i, h))]
    out_shape = [jax.ShapeDtypeStruct((batch * seq, n_heads * vd), bf16)]
    scratch = [
        pltpu.VMEM((2, tq, LANES), f32),
        pltpu.VMEM((2, tq, LANES), f32),
        pltpu.VMEM((2, tq, vd), f32),
    ]
    dec = None
    prefetch = []
    if decode is not None:
        d_in, d_args, d_out, d_shape, d_scratch, dec = _ride_along_operands(
            decode, batch * n_heads * nq, lambda b, h, i: (b * n_heads + h) * nq + i)
        in_specs += d_in
        args += d_args
        out_specs.append(d_out)
        out_shape.append(d_shape)
        scratch += d_scratch
        prefetch = [decode["page_table"]]
    par = "parallel" if dec is None else "arbitrary"
    res = pl.pallas_call(
        functools.partial(_attn_prompt_kernel, tk=tk, hd=hd, lam_init=lam_init, dec=dec),
        grid_spec=pltpu.PrefetchScalarGridSpec(
            num_scalar_prefetch=len(prefetch), grid=(batch, n_heads, nq), in_specs=in_specs,
            out_specs=out_specs, scratch_shapes=scratch),
        out_shape=out_shape,
        compiler_params=_cparams(par, par, "arbitrary"),
        name="attn_prompt" if dec is None else "attn_decode_ride",
    )(*prefetch, *args)
    return res if len(res) > 1 else res[0]


def _decode_step(lam_ref, gsub_ref, q_ref, kn_ref, vn_ref, k_refs, v_refs, o_ref,
                 m_ref, l_ref, acc_ref, *, first, active, lam_init):
    nh, w = q_ref.shape[1], q_ref.shape[2]
    hd = w // 2
    rows = k_refs[0].shape[2]
    nrep_w = w // LANES

    q = q_ref[0]
    lane = lax.broadcasted_iota(jnp.int32, (nh, w), 1)
    qm = jnp.concatenate([jnp.where(lane < hd, q, 0.0), jnp.where(lane >= hd, q, 0.0)], axis=0)
    r_i = lax.broadcasted_iota(jnp.int32, (2 * nh, rows), 0)
    c_i = lax.broadcasted_iota(jnp.int32, (2 * nh, rows), 1)
    valid = (c_i % nh) == jnp.where(active, r_i % nh, -1)
    scores = []
    for k_ref in k_refs:
        s = lax.dot_general(qm, k_ref[0, 0], (((1,), (1,)), ((), ())), preferred_element_type=f32)
        scores.append(jnp.where(valid, s, -jnp.inf))
    mc = scores[0]
    for s in scores[1:]:
        mc = jnp.maximum(mc, s)
    m_prev = jnp.where(first, -jnp.inf, m_ref[...])
    l_prev = jnp.where(first, 0.0, l_ref[...])
    acc_prev = jnp.where(first, 0.0, acc_ref[...])
    m_new = jnp.maximum(m_prev, jnp.max(mc, axis=1, keepdims=True))
    alpha = jnp.exp2(m_prev - m_new)
    m_b = _lane_repeat(m_new, rows // LANES)
    lsum = None
    pv = None
    for s, v_ref in zip(scores, v_refs):
        p = jnp.exp2(s - m_b)
        ls = jnp.sum(p, axis=1, keepdims=True)
        d = jnp.dot(p, v_ref[0, 0], preferred_element_type=f32)
        lsum = ls if lsum is None else lsum + ls
        pv = d if pv is None else pv + d
    l_new = alpha * l_prev + lsum
    acc_new = _lane_repeat(alpha, nrep_w) * acc_prev + pv
    m_ref[...] = m_new
    l_ref[...] = l_new
    acc_ref[...] = acc_new

    kn = jnp.concatenate([kn_ref[0], kn_ref[0]], axis=0)
    vn = jnp.concatenate([vn_ref[0], vn_ref[0]], axis=0)
    s_self = jnp.sum(qm * kn, axis=1, keepdims=True)
    m_f = jnp.maximum(m_new, s_self)
    a_f = jnp.exp2(m_new - m_f)
    p_f = jnp.exp2(s_self - m_f)
    l_f = a_f * l_new + p_f
    acc_f = _lane_repeat(a_f, nrep_w) * acc_new + _lane_repeat(p_f, nrep_w) * vn
    o = acc_f / _lane_repeat(l_f, nrep_w)
    d = o[0:nh] - _lambda(lam_ref, lam_init) * o[nh:2 * nh]
    o_ref[0] = (_rms(d, gsub_ref[...]) * (1.0 - lam_init)).astype(o_ref.dtype)


def _attn_decode_kernel(pt_ref, lam_ref, gsub_ref, q_ref, kn_ref, vn_ref, *rest, pp, lam_init):
    k_refs, v_refs = rest[:pp], rest[pp:2 * pp]
    o_ref, m_ref, l_ref, acc_ref = rest[2 * pp:]
    g = pl.program_id(1)

    @pl.when(jnp.logical_and(pl.program_id(0) == 0, g == 0))
    def _():
        m_ref[...] = jnp.full(m_ref.shape, -jnp.inf, f32)
        l_ref[...] = jnp.zeros(l_ref.shape, f32)
        acc_ref[...] = jnp.zeros(acc_ref.shape, f32)

    _decode_step(lam_ref, gsub_ref, q_ref, kn_ref, vn_ref, k_refs, v_refs, o_ref,
                 m_ref, l_ref, acc_ref, first=g == 0, active=True, lam_init=lam_init)


def _attn_decode(decode, pp=8):
    q, cache_k, cache_v, page_table = (decode[k] for k in ("q", "cache_k", "cache_v", "page_table"))
    layer, batch0, nbatch = decode["layer"], decode["batch0"], decode["nbatch"]
    _, nh, w = q.shape
    hd = w // 2
    n_pages = page_table.shape[1]
    pp = _pick(n_pages, pp)
    page_rows = cache_k.shape[2]

    def page_spec(pi):
        return pl.BlockSpec((1, 1, page_rows, w),
                            lambda i, g, pt, pi=pi: (layer, pt[batch0 + i, g * pp + pi], 0, 0))

    row_in = pl.BlockSpec((1, nh, w), lambda i, g, pt: (batch0 + i, 0, 0))
    grid_spec = pltpu.PrefetchScalarGridSpec(
        num_scalar_prefetch=1,
        grid=(nbatch, n_pages // pp),
        in_specs=[
            pl.BlockSpec((4, hd), lambda i, g, pt: (0, 0)),
            pl.BlockSpec((1, w), lambda i, g, pt: (0, 0)),
            row_in, row_in, row_in,
        ] + [page_spec(pi) for pi in range(pp)] * 2,
        out_specs=pl.BlockSpec((1, nh, w), lambda i, g, pt: (i, 0, 0)),
        scratch_shapes=[
            pltpu.VMEM((2 * nh, LANES), f32),
            pltpu.VMEM((2 * nh, LANES), f32),
            pltpu.VMEM((2 * nh, w), f32),
        ],
    )
    return pl.pallas_call(
        functools.partial(_attn_decode_kernel, pp=pp, lam_init=decode["lam_init"]),
        grid_spec=grid_spec,
        out_shape=jax.ShapeDtypeStruct((nbatch, nh, w), bf16),
        compiler_params=_cparams("arbitrary", "arbitrary"),
        name="attn_decode",
    )(page_table, decode["lam4"], decode["g_sub"], q, decode["k_new"], decode["v_new"],
      *([cache_k] * pp), *([cache_v] * pp))


def _lru_gates(xc, wa_ref, ba_ref, wx_ref, bx_ref, ap_ref):
    nb, blk = wa_ref.shape[0], wa_ref.shape[1]
    xcb = xc.astype(bf16)
    za = jnp.concatenate(
        [jnp.dot(xcb[:, n * blk:(n + 1) * blk], wa_ref[n], preferred_element_type=f32)
         for n in range(nb)], axis=1)
    zx = jnp.concatenate(
        [jnp.dot(xcb[:, n * blk:(n + 1) * blk], wx_ref[n], preferred_element_type=f32)
         for n in range(nb)], axis=1)
    gate_a = jax.nn.sigmoid(za + ba_ref[...])
    gate_x = jax.nn.sigmoid(zx + bx_ref[...])
    ap = -ap_ref[...]
    softplus = jnp.maximum(ap, 0.0) + jnp.log1p(jnp.exp(-jnp.abs(ap)))
    log_a = (-LRU_C) * gate_a * softplus
    a = jnp.exp(log_a)
    mult = jnp.sqrt(-jnp.tanh(log_a) * (a * a + 1.0))
    return a, gate_x, mult


def _lru_prompt_kernel(*refs, tt, pos0, dec):
    ((xr_ref, xg_ref, cw_ref, cb_ref, wa_ref, ba_ref, wx_ref, bx_ref, ap_ref),
     (o_ref, conv_ref, h_ref), (xprev_ref, hc_ref, a_s, b_s), side) = _split_refs(refs, 9, 3, dec)
    t = pl.program_id(1)
    nt = pl.num_programs(1)
    w = xr_ref.shape[1]
    cwid = cw_ref.shape[0]
    assert cwid <= SUBLANES + 1
    ng = tt // SUBLANES

    @pl.when(t == 0)
    def _():
        xprev_ref[...] = jnp.zeros(xprev_ref.shape, f32)
        hc_ref[...] = jnp.zeros(hc_ref.shape, f32)

    if dec is not None:
        _ride_along(dec, side, pl.program_id(0) * nt + t)

    x = xr_ref[...].reshape(ng, SUBLANES, w)
    prev = xprev_ref[...]
    sub = lax.broadcasted_iota(jnp.int32, (ng, SUBLANES, w), 1)
    xc = x * cw_ref[cwid - 1:cwid, :]
    for i in range(1, cwid):
        r = pltpu.roll(x, i, axis=1)
        r_prev = jnp.concatenate([pltpu.roll(prev, i, axis=0)[None], r[:ng - 1]], axis=0)
        xc = xc + jnp.where(sub >= i, r, r_prev) * cw_ref[cwid - 1 - i:cwid - i, :]
    xc = (xc + cb_ref[...]).reshape(tt, w)
    xprev_ref[...] = x[ng - 1]

    a, gate_x, mult = _lru_gates(xc, wa_ref, ba_ref, wx_ref, bx_ref, ap_ref)
    if pos0 == 0:
        row = lax.broadcasted_iota(jnp.int32, (tt, w), 0)
        mult = jnp.where((row + t * tt) == 0, 1.0, mult)
    b = xc * gate_x * mult

    a = a.reshape(ng, SUBLANES, w)
    b = b.reshape(ng, SUBLANES, w)
    for s in (1, 2, 4):
        keep = sub >= s
        b = jnp.where(keep, a * pltpu.roll(b, s, axis=1) + b, b)
        a = jnp.where(keep, a * pltpu.roll(a, s, axis=1), a)
    a_s[...] = a.reshape(tt, w)
    b_s[...] = b.reshape(tt, w)

    def grp(gi, h):
        r0 = pl.multiple_of(gi * SUBLANES, SUBLANES)
        hg = a_s[pl.ds(r0, SUBLANES), :] * h + b_s[pl.ds(r0, SUBLANES), :]
        b_s[pl.ds(r0, SUBLANES), :] = hg
        return jnp.broadcast_to(hg[SUBLANES - 1:SUBLANES, :], (SUBLANES, w))

    h = lax.fori_loop(0, tt // SUBLANES, grp, hc_ref[...])
    hc_ref[...] = h
    o_ref[...] = (b_s[...] * _gelu_tanh(xg_ref[...].astype(f32))).astype(o_ref.dtype)

    @pl.when(t == nt - 1)
    def _():
        conv_ref[0] = xprev_ref[SUBLANES - (cwid - 1):SUBLANES, :]
        h_ref[0] = h[0:1, :]


def _lru_grid(batch, seq):
    return batch, seq // _pick(seq, LRU_TILE_ROWS)


def _lru_prompt(xr, xg, conv_w, conv_b, w_a, b_a, w_x, b_x, a_param, *, batch, seq, pos0,
                decode=None):
    w = xr.shape[1]
    _, nt = _lru_grid(batch, seq)
    tt = seq // nt
    cwid = conv_w.shape[0]
    vec = pl.BlockSpec((1, w), lambda b, t, *_: (0, 0))
    wspec = pl.BlockSpec(w_a.shape, lambda b, t, *_: (0, 0, 0))
    in_specs = [
        pl.BlockSpec((tt, w), lambda b, t, *_: (b * nt + t, 0)),
        pl.BlockSpec((tt, w), lambda b, t, *_: (b * nt + t, 0)),
        pl.BlockSpec((cwid, w), lambda b, t, *_: (0, 0)),
        vec, wspec, vec, wspec, vec, vec,
    ]
    args = [xr, xg, conv_w, conv_b, w_a, b_a, w_x, b_x, a_param]
    out_specs = [
        pl.BlockSpec((tt, w), lambda b, t, *_: (b * nt + t, 0)),
        pl.BlockSpec((1, cwid - 1, w), lambda b, t, *_: (b, 0, 0)),
        pl.BlockSpec((1, 1, w), lambda b, t, *_: (b, 0, 0)),
    ]
    out_shape = [
        jax.ShapeDtypeStruct((batch * seq, w), bf16),
        jax.ShapeDtypeStruct((batch, cwid - 1, w), f32),
        jax.ShapeDtypeStruct((batch, 1, w), f32),
    ]
    scratch = [
        pltpu.VMEM((SUBLANES, w), f32),
        pltpu.VMEM((SUBLANES, w), f32),
        pltpu.VMEM((tt, w), f32),
        pltpu.VMEM((tt, w), f32),
    ]
    dec = None
    prefetch = []
    if decode is not None:
        d_in, d_args, d_out, d_shape, d_scratch, dec = _ride_along_operands(
            decode, batch * nt, lambda b, t: b * nt + t)
        in_specs += d_in
        args += d_args
        out_specs.append(d_out)
        out_shape.append(d_shape)
        scratch += d_scratch
        prefetch = [decode["page_table"]]
    return pl.pallas_call(
        functools.partial(_lru_prompt_kernel, tt=tt, pos0=pos0, dec=dec),
        grid_spec=pltpu.PrefetchScalarGridSpec(
            num_scalar_prefetch=len(prefetch), grid=(batch, nt), in_specs=in_specs,
            out_specs=out_specs, scratch_shapes=scratch),
        out_shape=out_shape,
        compiler_params=pltpu.CompilerParams(
            dimension_semantics=("parallel" if dec is None else "arbitrary", "arbitrary"),
            vmem_limit_bytes=LRU_VMEM_LIMIT_BYTES),
        name="lru_prompt" if dec is None else "lru_decode",
    )(*prefetch, *args)


def _lru_step_kernel(xr_ref, xg_ref, buf_ref, h0_ref, cw_ref, cb_ref, wa_ref, ba_ref, wx_ref,
                     bx_ref, ap_ref, o_ref, conv_ref, h_ref, *, pos0):
    cwid = cw_ref.shape[0]
    x = xr_ref[...]
    xc = cb_ref[...] + x * cw_ref[cwid - 1:cwid, :]
    for i in range(cwid - 1):
        xc = xc + buf_ref[i] * cw_ref[i:i + 1, :]
        if i > 0:
            conv_ref[i - 1] = buf_ref[i]
    conv_ref[cwid - 2] = x
    a, gate_x, mult = _lru_gates(xc, wa_ref, ba_ref, wx_ref, bx_ref, ap_ref)
    if pos0 == 0:
        mult = jnp.ones_like(mult)
    h = a * h0_ref[...] + xc * gate_x * mult
    h_ref[...] = h
    o_ref[...] = (h * _gelu_tanh(xg_ref[...].astype(f32))).astype(o_ref.dtype)


def _lru_step(xr, xg, buf, h0, conv_w, conv_b, w_a, b_a, w_x, b_x, a_param, *, pos0):
    b, w = xr.shape
    cwid = conv_w.shape[0]
    vec = pl.BlockSpec((1, w), lambda i: (0, 0))
    wspec = pl.BlockSpec(w_a.shape, lambda i: (0, 0, 0))
    return pl.pallas_call(
        functools.partial(_lru_step_kernel, pos0=pos0),
        grid=(1,),
        in_specs=[
            pl.BlockSpec((b, w), lambda i: (0, 0)),
            pl.BlockSpec((b, w), lambda i: (0, 0)),
            pl.BlockSpec((cwid - 1, b, w), lambda i: (0, 0, 0)),
            pl.BlockSpec((b, w), lambda i: (0, 0)),
            pl.BlockSpec((cwid, w), lambda i: (0, 0)),
            vec, wspec, vec, wspec, vec, vec,
        ],
        out_specs=[
            pl.BlockSpec((b, w), lambda i: (0, 0)),
            pl.BlockSpec((cwid - 1, b, w), lambda i: (0, 0, 0)),
            pl.BlockSpec((b, w), lambda i: (0, 0)),
        ],
        out_shape=[
            jax.ShapeDtypeStruct((b, w), bf16),
            jax.ShapeDtypeStruct((cwid - 1, b, w), f32),
            jax.ShapeDtypeStruct((b, w), f32),
        ],
        compiler_params=_cparams("arbitrary"),
        name="lru_step",
    )(xr, xg, buf, h0, conv_w, conv_b, w_a, b_a, w_x, b_x, a_param)


def _merge_kernel(a_ref, b_ref, wpa_ref, wpb_ref, ga_ref, gb_ref, wo_ref, x_ref, o_ref):
    j = pl.program_id(1)

    @pl.when(j == 0)
    def _():
        o_ref[...] = x_ref[...]

    ya = jnp.dot(a_ref[...], wpa_ref[...], preferred_element_type=f32)
    yb = jnp.dot(b_ref[...], wpb_ref[...], preferred_element_type=f32)
    y = ga_ref[...].astype(f32) * ya + gb_ref[...].astype(f32) * yb
    o_ref[...] += jnp.dot(y.astype(bf16), wo_ref[...], preferred_element_type=f32)


def _merge(a, b, w_pa, w_pb, gates, w_o, x, *, bm=512, bn=512):
    m, d = x.shape
    ka = a.shape[1]
    kb = b.shape[1]
    bm = _pick(m, bm)
    bn = _pick(d, bn)
    nj = d // bn
    return pl.pallas_call(
        _merge_kernel,
        grid=(m // bm, nj),
        in_specs=[
            pl.BlockSpec((bm, ka), lambda i, j: (i, 0)),
            pl.BlockSpec((bm, kb), lambda i, j: (i, 0)),
            pl.BlockSpec((ka, bn), lambda i, j: (0, j)),
            pl.BlockSpec((kb, bn), lambda i, j: (0, j)),
            pl.BlockSpec((bm, bn), lambda i, j: (i, j)),
            pl.BlockSpec((bm, bn), lambda i, j: (i, j + nj)),
            pl.BlockSpec((bn, d), lambda i, j: (j, 0)),
            pl.BlockSpec((bm, d), lambda i, j: (i, 0)),
        ],
        out_specs=pl.BlockSpec((bm, d), lambda i, j: (i, 0)),
        out_shape=jax.ShapeDtypeStruct((m, d), f32),
        compiler_params=_cparams("parallel", "arbitrary"),
        name="merge",
    )(a, b, w_pa, w_pb, gates, gates, w_o, x)


def _rope_tables(positions, hd):
    half = hd // 2
    inv = 1.0 / (ROPE_THETA ** (jnp.arange(half, dtype=f32) / half))
    ang = positions.astype(f32)[:, None] * inv[None, :]
    cos, sin = jnp.cos(ang), jnp.sin(ang)
    return jnp.concatenate([cos, cos], axis=-1), jnp.concatenate([-sin, sin], axis=-1)


def _lambda_init(layer):
    return 0.8 - 0.6 * math.exp(-0.3 * layer)


def kernel(x_prompt, x_sample, cache_k, cache_v, state_conv, state_h, page_table, g_ffn1, w_up1, w_down1, g_mix, w_in, g_q, g_k, lam_q1, lam_k1, lam_q2, lam_k2, g_sub, conv_w, conv_b, w_a, b_a, w_x, b_x, a_param, w_pa, w_pb, w_o, g_ffn2, w_up2, w_down2):
    batch, seq, d = x_prompt.shape
    dec_b, dec_t, _ = x_sample.shape
    assert dec_t == 1, "sample group decodes one token per sequence"
    depth, n_pool, page, n_heads, kw = cache_k.shape
    hd = g_q.shape[1]
    vd = g_sub.shape[1]
    assert kw == 2 * hd and vd == 2 * hd
    lw = conv_w.shape[2]
    qk_w = n_heads * 2 * hd
    at_w = n_heads * vd
    past = page_table.shape[1] * page
    scale = hd ** -0.5 * math.log2(math.e)

    rope_p = _rope_tables(jnp.arange(seq), hd)
    rope_s = _rope_tables(jnp.full((dec_b,), past), hd)

    def fold(rope, g, s):
        return rope[0] * (g * s), rope[1] * (jnp.roll(g, hd // 2, axis=-1) * s)

    yp = x_prompt.reshape(batch * seq, d)
    ys = x_sample.reshape(dec_b, d)
    outs = [[] for _ in range(8)]
    for l in range(depth):
        lam_init = _lambda_init(l)
        wup1, wdn1 = w_up1[l].astype(bf16), w_down1[l].astype(bf16)
        wup2, wdn2 = w_up2[l].astype(bf16), w_down2[l].astype(bf16)
        win = w_in[l]
        wpa, wpb, wo = w_pa[l].astype(bf16), w_pb[l].astype(bf16), w_o[l].astype(bf16)
        wa, wx = w_a[l].astype(bf16), w_x[l].astype(bf16)
        lam4 = jnp.concatenate([lam_q1[l][None], lam_k1[l][None], lam_q2[l][None], lam_k2[l][None]], 0)
        gsub, gq, gk = g_sub[l][None], g_q[l][None], g_k[l][None]
        lru_w = (conv_w[l], conv_b[l][None], wa, b_a[l][None], wx, b_x[l][None], a_param[l][None])
        ck = cache_k.reshape(depth, n_pool, page * n_heads, kw)
        cv = cache_v.reshape(depth, n_pool, page * n_heads, vd)

        def projections(hn, rope, q_dtype):
            (q,) = _proj(hn, win, 0, qk_w, [q_dtype], qk=fold(rope, gq, scale))
            k32, k16 = _proj(hn, win, qk_w, qk_w, [f32, bf16], qk=fold(rope, gk, 1.0))
            v32, v16 = _proj(hn, win, 2 * qk_w, at_w, [f32, bf16])
            (xr,) = _proj(hn, win, 2 * qk_w + at_w, lw, [f32])
            (xg,) = _proj(hn, win, 2 * qk_w + at_w + lw, lw, [bf16])
            (gates,) = _proj(hn, win, 2 * qk_w + at_w + 2 * lw, 2 * d, [bf16], act="sigmoid")
            return q, k32, k16, v32, v16, xr, xg, gates

        x1_s, hn_s = _ffn(ys, g_ffn1[l][None], wup1, wdn1, g_mix[l][None])
        q_s, k32_s, _, v32_s, _, xr_s, xg_s, gates_s = projections(hn_s, rope_s, f32)
        n_pages = page_table.shape[1]
        dec = dict(q=q_s.reshape(dec_b, n_heads, kw), k_new=k32_s.reshape(dec_b, n_heads, kw),
                   v_new=v32_s.reshape(dec_b, n_heads, vd), cache_k=ck, cache_v=cv,
                   page_table=page_table, lam4=lam4, g_sub=gsub, layer=l, lam_init=lam_init)
        pp_lru = _pick(n_pages, DECODE_PAGES_PER_LRU_STEP)
        pp_att = _pick(n_pages, DECODE_PAGES_PER_ATTN_STEP)
        pp_ffn = _pick(n_pages, DECODE_PAGES_PER_FFN_STEP)
        lb, lt = _lru_grid(batch, seq)
        ab, ah, aq = _attn_grid(batch, seq, n_heads)
        ni, nj = _ffn_grid(batch * seq, wdn1.shape[0])
        n_lru = min(dec_b, (lb * lt) // (n_pages // pp_lru))
        n_att = min(dec_b - n_lru, (ab * ah * aq) // (n_pages // pp_att))
        fit = (ni * nj) // (n_pages // pp_ffn)
        n1 = min(fit, dec_b - n_lru - n_att)
        n2 = min(fit, dec_b - n_lru - n_att - n1)
        n3 = dec_b - n_lru - n_att - n1 - n2
        att_s = []

        def host(fn, *args, nbatch, pp, **kw):
            if not nbatch:
                return fn(*args, **kw)
            batch0 = sum(a.shape[0] for a in att_s)
            *res, att_part = fn(*args, decode=dict(dec, batch0=batch0, nbatch=nbatch, pp=pp), **kw)
            att_s.append(att_part)
            return res[0] if len(res) == 1 else res

        x1, hn = host(_ffn, yp, g_ffn1[l][None], wup1, wdn1, g_mix[l][None], nbatch=n1, pp=pp_ffn)
        q, k32, k16, v32, v16, xr, xg, gates = projections(hn, rope_p, bf16)
        att = host(_attn_prompt, q, k16, v16, lam4, gsub, nbatch=n_att, pp=pp_att, batch=batch,
                   seq=seq, n_heads=n_heads, hd=hd, vd=vd, lam_init=lam_init)
        lru, conv_p, h_p = host(_lru_prompt, xr, xg, *lru_w, nbatch=n_lru, pp=pp_lru,
                                batch=batch, seq=seq, pos0=0)
        x2 = _merge(att, lru, wpa, wpb, gates, wo, x1)
        yp = host(_ffn, x2, g_ffn2[l][None], wup2, wdn2, nbatch=n2, pp=pp_ffn)
        if n3:
            batch0 = sum(a.shape[0] for a in att_s)
            att_s.append(_attn_decode(dict(dec, batch0=batch0, nbatch=n3)))
        outs[0].append(k32.reshape(batch, seq, n_heads, kw))
        outs[1].append(v32.reshape(batch, seq, n_heads, vd))
        outs[2].append(conv_p)
        outs[3].append(h_p.reshape(batch, lw))

        att = jnp.concatenate(att_s, axis=0) if len(att_s) > 1 else att_s[0]
        lru, conv_s, h_s = _lru_step(xr_s, xg_s, jnp.moveaxis(state_conv[l], 1, 0), state_h[l],
                                     *lru_w, pos0=past)
        x2 = _merge(att.reshape(dec_b, at_w), lru, wpa, wpb, gates_s, wo, x1_s)
        ys = _ffn(x2, g_ffn2[l][None], wup2, wdn2)
        outs[4].append(k32_s.reshape(dec_b, dec_t, n_heads, kw))
        outs[5].append(v32_s.reshape(dec_b, dec_t, n_heads, vd))
        outs[6].append(jnp.moveaxis(conv_s, 0, 1))
        outs[7].append(h_s)

    return (yp.reshape(batch, seq, d), ys.reshape(dec_b, dec_t, d),
            *[jnp.stack(o) for o in outs])
```
